```python
import math
import jax, jax.numpy as jnp
from jax import lax
import numpy as np

D_MODEL = 1024
BATCH = 2
SEQ = 8192
DEPTH = 4
DEC_BATCH = 32
DEC_SEQ = 4
PAST_LEN = 8192
PAGE_SIZE = 128

HEAD_DIM = 64
N_HEADS = D_MODEL // HEAD_DIM
H_MOBA = N_HEADS // 2
H_FOX = N_HEADS - H_MOBA
MOBA_BLOCK = 256
MOBA_TOPK = 3
NSA_KV_HEADS = 4
NSA_GROUP = N_HEADS // NSA_KV_HEADS
CMP_LEN = 32
CMP_STRIDE = 16
SEL_BLOCK = 64
SEL_TOPN = 16
WINDOW = 512
N_BUCKETS = 32
MAX_DISTANCE = 4096
FORGET_BIAS = 4.0
D_FF = ((8 * D_MODEL // 3) + 127) // 128 * 128
N_EXPERTS = 8
TOP_K = 2
Q_BLOCK = 128
N_EVEN = (DEPTH + 1) // 2
N_ODD = DEPTH // 2
ALPHA = (2 * DEPTH) ** 0.25
BETA = (8 * DEPTH) ** -0.25
EVEN_COLS = 3 * N_HEADS * HEAD_DIM + H_FOX
ODD_COLS = N_HEADS * HEAD_DIM + 6 * NSA_KV_HEADS * HEAD_DIM + 3 * N_HEADS
NEG = -1e30
FORCED_SCORE = 1e6
LN_EPS = 1e-5

kernel_name = "moba_fox_nsa_hybrid_decode_step"


def layer_norm(x, g, b):
    xf = x.astype(jnp.float32)
    mu = jnp.mean(xf, -1, keepdims=True)
    var = jnp.mean(jnp.square(xf - mu), -1, keepdims=True)
    return ((xf - mu) * lax.rsqrt(var + LN_EPS) * g + b).astype(x.dtype)


def rel_bucket(dist):
    n = jnp.maximum(dist, 0)
    max_exact = N_BUCKETS // 2
    nf = jnp.maximum(n, 1).astype(jnp.float32)
    large = max_exact + (jnp.log(nf / max_exact) / math.log(MAX_DISTANCE / max_exact)
                         * (N_BUCKETS - max_exact)).astype(jnp.int32)
    large = jnp.minimum(large, N_BUCKETS - 1)
    return jnp.where(n < max_exact, n, large)


def masked_probs(s, mask):
    s = jnp.where(mask, s, NEG)
    m = jnp.max(s, -1, keepdims=True)
    p = jnp.where(mask, jnp.exp(s - m), 0.0)
    return p / jnp.maximum(jnp.sum(p, -1, keepdims=True), 1e-30)


def block_map(fn, T):
    qb = min(Q_BLOCK, T)
    out = lax.map(lambda i0: fn(i0, qb), jnp.arange(T // qb) * qb)
    out = jnp.moveaxis(out, 0, 1)
    return out.reshape((out.shape[0], T) + out.shape[3:])


def moba_attention(q, k, v, rel_tab, pos0):
    B, T, H, D = q.shape
    L = k.shape[1]
    nblk = max(-(-L // MOBA_BLOCK), MOBA_TOPK + 1)
    pad = nblk * MOBA_BLOCK - L
    kb = jnp.pad(k, ((0, 0), (0, pad), (0, 0), (0, 0))).reshape(B, nblk, MOBA_BLOCK, H, D).transpose(0, 3, 1, 2, 4)
    vb = jnp.pad(v, ((0, 0), (0, pad), (0, 0), (0, 0))).reshape(B, nblk, MOBA_BLOCK, H, D).transpose(0, 3, 1, 2, 4)
    kmean = jnp.mean(kb.astype(jnp.float32), axis=3)
    bidx = jnp.arange(nblk)
    bi = jnp.arange(B)[:, None, None, None]
    hi = jnp.arange(H)[None, :, None, None]
    hi5 = jnp.arange(H)[None, :, None, None, None]
    scale = D ** -0.5

    def blk(i0, qb):
        qi = lax.dynamic_slice_in_dim(q, i0, qb, 1).transpose(0, 2, 1, 3)
        qpos = pos0 + i0 + jnp.arange(qb)
        own = qpos // MOBA_BLOCK
        gsc = jnp.einsum('bhqd,bhnd->bhqn', qi.astype(jnp.float32), kmean)
        gsc = jnp.where(bidx[None, :] < own[:, None], gsc, NEG)
        _, sel = lax.top_k(gsc, MOBA_TOPK)
        sel_ok = sel < own[:, None]
        idx = jnp.concatenate([sel, jnp.broadcast_to(own[:, None], (B, H, qb, 1))], -1)
        ok_blk = jnp.concatenate([sel_ok, jnp.ones((B, H, qb, 1), bool)], -1)
        kg = kb[bi, hi, idx]
        vg = vb[bi, hi, idx]
        s = jnp.einsum('bhqd,bhqjkd->bhqjk', qi, kg, preferred_element_type=jnp.float32) * scale
        kpos = idx[..., None] * MOBA_BLOCK + jnp.arange(MOBA_BLOCK)
        dist = qpos[:, None, None] - kpos
        s = s + rel_tab[hi5, rel_bucket(dist)]
        ok = ok_blk[..., None] & (dist >= 0)
        p = masked_probs(s.reshape(B, H, qb, -1), ok.reshape(B, H, qb, -1)).reshape(s.shape)
        return jnp.einsum('bhqjk,bhqjkd->bqhd', p.astype(v.dtype), vg)

    return block_map(blk, T)


def fox_attention(q, k, v, logf, pos0):
    B, T, H, D = q.shape
    L = k.shape[1]
    c = jnp.cumsum(logf.astype(jnp.float32), axis=1)
    c_t = c.transpose(0, 2, 1)
    cq = c[:, L - T:]
    kpos = jnp.arange(L)
    scale = D ** -0.5

    def blk(i0, qb):
        qi = lax.dynamic_slice_in_dim(q, i0, qb, 1)
        ci = lax.dynamic_slice_in_dim(cq, i0, qb, 1).transpose(0, 2, 1)
        s = jnp.einsum('bqhd,bkhd->bhqk', qi, k, preferred_element_type=jnp.float32) * scale
        s = s + ci[..., None] - c_t[:, :, None, :]
        qpos = pos0 + i0 + jnp.arange(qb)
        p = masked_probs(s, kpos[None, :] <= qpos[:, None])
        return jnp.einsum('bhqk,bkhd->bqhd', p.astype(v.dtype), v)

    return block_map(blk, T)


def compress(k, pe, w1, w2):
    B, L, Hk, D = k.shape
    n_sub = CMP_LEN // CMP_STRIDE
    nchunk = L // CMP_STRIDE
    ch = k[:, :nchunk * CMP_STRIDE].reshape(B, nchunk, CMP_STRIDE, Hk, D)
    nc = nchunk - n_sub + 1
    blocks = jnp.concatenate([ch[:, i:i + nc] for i in range(n_sub)], axis=2)
    h = jnp.einsum('bnlhd,lde->bnhe', blocks + pe[:, None, :], w1)
    return jnp.einsum('bnhe,ef->bnhf', jax.nn.silu(h), w2)


def nsa_attention(q, kc, vc, ks, vs, kw, vw, gates, rel_tab, pos0):
    B, T, Hk, G, D = q.shape
    L = ks.shape[1]
    nc = kc.shape[1]
    nsel = max(-(-L // SEL_BLOCK), SEL_TOPN)
    pad = nsel * SEL_BLOCK - L
    ksb = jnp.pad(ks, ((0, 0), (0, pad), (0, 0), (0, 0))).reshape(B, nsel, SEL_BLOCK, Hk, D).transpose(0, 3, 1, 2, 4)
    vsb = jnp.pad(vs, ((0, 0), (0, pad), (0, 0), (0, 0))).reshape(B, nsel, SEL_BLOCK, Hk, D).transpose(0, 3, 1, 2, 4)
    cend = jnp.arange(nc) * CMP_STRIDE + CMP_LEN - 1
    tab = rel_tab.reshape(Hk, G, N_BUCKETS)
    r = SEL_BLOCK // CMP_STRIDE
    sel_ar = jnp.arange(nsel)
    bi = jnp.arange(B)[:, None, None, None]
    ki = jnp.arange(Hk)[None, :, None, None]
    k6 = jnp.arange(Hk)[None, :, None, None, None, None]
    g6 = jnp.arange(G)[None, None, :, None, None, None]
    scale = D ** -0.5

    def blk(i0, qb):
        qi = lax.dynamic_slice_in_dim(q, i0, qb, 1)
        gi = lax.dynamic_slice_in_dim(gates, i0, qb, 1)
        qpos = pos0 + i0 + jnp.arange(qb)
        sc = jnp.einsum('bqkgd,bnkd->bkgqn', qi, kc, preferred_element_type=jnp.float32) * scale
        pc = masked_probs(sc, cend[None, :] <= qpos[:, None])
        o_cmp = jnp.einsum('bkgqn,bnkd->bqkgd', pc.astype(vc.dtype), vc)
        ppad = jnp.pad(pc, ((0, 0), (0, 0), (0, 0), (0, 0), (1, r * (nsel + 1) - 1 - nc)))
        p4 = ppad.reshape(B, Hk, G, qb, nsel + 1, r)
        pslc = jnp.sum(jnp.sum(p4[..., :nsel, :], -1) + p4[..., 1:, 0], axis=2)
        own = qpos // SEL_BLOCK
        valid = sel_ar[None, :] <= own[:, None]
        forced = (sel_ar[None, :] == 0) | (sel_ar[None, :] == own[:, None]) | (sel_ar[None, :] == own[:, None] - 1)
        score = jnp.where(valid, jnp.where(forced, FORCED_SCORE, pslc), NEG)
        _, idx = lax.top_k(score, SEL_TOPN)
        ksg = ksb[bi, ki, idx]
        vsg = vsb[bi, ki, idx]
        kpos = idx[..., None] * SEL_BLOCK + jnp.arange(SEL_BLOCK)
        dist = qpos[:, None, None] - kpos
        ss = jnp.einsum('bqkgd,bkqjsd->bkgqjs', qi, ksg, preferred_element_type=jnp.float32) * scale
        ss = ss + tab[k6, g6, rel_bucket(dist)[:, :, None]]
        ok = (dist >= 0)[:, :, None]
        ps = masked_probs(ss.reshape(B, Hk, G, qb, -1), ok.reshape(B, Hk, 1, qb, -1)).reshape(ss.shape)
        o_sel = jnp.einsum('bkgqjs,bkqjsd->bqkgd', ps.astype(vs.dtype), vsg)
        kwi = lax.dynamic_slice_in_dim(kw, i0, WINDOW + qb, 1)
        vwi = lax.dynamic_slice_in_dim(vw, i0, WINDOW + qb, 1)
        wpos = pos0 + i0 - WINDOW + jnp.arange(WINDOW + qb)
        wd = qpos[:, None] - wpos[None, :]
        wok = (wd >= 0) & (wd < WINDOW) & (wpos[None, :] >= 0)
        sw = jnp.einsum('bqkgd,bskd->bkgqs', qi, kwi, preferred_element_type=jnp.float32) * scale
        sw = sw + tab[:, :, rel_bucket(wd)]
        pw = masked_probs(sw, wok)
        o_win = jnp.einsum('bkgqs,bskd->bqkgd', pw.astype(vw.dtype), vwi)
        return gi[..., 0:1] * o_cmp + gi[..., 1:2] * o_sel + gi[..., 2:3] * o_win

    return block_map(blk, T)


def even_mixer(x, pos0, past_kv, past_logf, w_in, b_f, w_out, rel_tab):
    B, T, _ = x.shape
    nqkv = 3 * N_HEADS * HEAD_DIM
    proj = x @ w_in
    qkv = proj[..., :nqkv].reshape(B, T, 3, N_HEADS, HEAD_DIM)
    q, kv_new = qkv[:, :, 0], qkv[:, :, 1:]
    logf_new = jax.nn.log_sigmoid((proj[..., nqkv:] + b_f).astype(jnp.float32)).astype(x.dtype)
    if past_kv is None:
        kv_all, logf_all = kv_new, logf_new
    else:
        kv_all = jnp.concatenate([past_kv, kv_new], axis=1)
        logf_all = jnp.concatenate([past_logf, logf_new], axis=1)
    o_a = moba_attention(q[:, :, :H_MOBA], kv_all[:, :, 0, :H_MOBA], kv_all[:, :, 1, :H_MOBA], rel_tab[:H_MOBA], pos0)
    o_b = fox_attention(q[:, :, H_MOBA:], kv_all[:, :, 0, H_MOBA:], kv_all[:, :, 1, H_MOBA:], logf_all, pos0)
    o = jnp.concatenate([o_a, o_b], axis=2).reshape(B, T, N_HEADS * HEAD_DIM)
    return o @ w_out, kv_new, logf_new


def odd_mixer(x, pos0, past_kv, win_buf, w_in, b_gate, pe, w1, w2, w_out, rel_tab):
    B, T, _ = x.shape
    hk, d = NSA_KV_HEADS, HEAD_DIM
    nq = N_HEADS * d
    proj = x @ w_in
    q = proj[..., :nq].reshape(B, T, hk, NSA_GROUP, d)
    kv_new = proj[..., nq:nq + 4 * hk * d].reshape(B, T, 2, 2 * hk, d)
    kvw_new = proj[..., nq + 4 * hk * d:nq + 6 * hk * d].reshape(B, T, 2, hk, d)
    gates = jax.nn.sigmoid(proj[..., nq + 6 * hk * d:] + b_gate).reshape(B, T, hk, NSA_GROUP, 3)
    kv_all = kv_new if past_kv is None else jnp.concatenate([past_kv, kv_new], axis=1)
    if win_buf is None:
        prev = jnp.zeros((B, WINDOW, 2, hk, d), x.dtype)
        win_state = kvw_new[:, -min(WINDOW, T):]
    else:
        prev = jnp.pad(win_buf, ((0, 0), (WINDOW - win_buf.shape[1], 0), (0, 0), (0, 0), (0, 0)))
        win_state = jnp.concatenate([win_buf, kvw_new], axis=1)[:, -win_buf.shape[1]:]
    ext = jnp.concatenate([prev, kvw_new], axis=1)
    kc = compress(kv_all[:, :, 0, :hk], pe[0], w1[0], w2[0])
    vc = compress(kv_all[:, :, 1, :hk], pe[1], w1[1], w2[1])
    o = nsa_attention(q, kc, vc, kv_all[:, :, 0, hk:], kv_all[:, :, 1, hk:], ext[:, :, 0], ext[:, :, 1],
                      gates, rel_tab, pos0)
    return o.reshape(B, T, nq) @ w_out, kv_new, win_state


def swiglu(x, w13, w2):
    a, b = jnp.split(x @ w13, 2, axis=-1)
    return (jax.nn.silu(a) * b) @ w2


def moe_swiglu(x, w_router, w13, w2):
    logits = (x @ w_router).astype(jnp.float32)
    top_v, top_i = lax.top_k(logits, TOP_K)
    g = jax.nn.softmax(top_v, axis=-1)
    dense_gate = jnp.sum(jax.nn.one_hot(top_i, N_EXPERTS, dtype=jnp.float32) * g[..., None], axis=-2)
    y = jnp.zeros_like(x)
    for e in range(N_EXPERTS):
        y = y + dense_gate[..., e:e + 1].astype(x.dtype) * swiglu(x, w13[e], w2[e])
    return y


def setup_inputs(seed: int = 0) -> dict:
    key = jax.random.key(seed)
    ks = jax.random.split(key, 24)
    f32 = jnp.float32
    n_pages = PAST_LEN // PAGE_SIZE
    n_pool = (DEC_BATCH * n_pages * 5 + 3) // 4
    win_buf = min(WINDOW, PAST_LEN)

    def nrm(k, shape, scale=1.0):
        return jax.random.normal(k, shape, f32) * scale

    d = D_MODEL
    return {
        "x_prompt": nrm(ks[0], (BATCH, SEQ, d)),
        "x_sample": nrm(ks[1], (DEC_BATCH, DEC_SEQ, d)),
        "cache_even_kv": nrm(ks[2], (n_pool, N_EVEN, PAGE_SIZE, 2, N_HEADS, HEAD_DIM)),
        "cache_even_logf": jax.nn.log_sigmoid(FORGET_BIAS + nrm(ks[3], (n_pool, N_EVEN, PAGE_SIZE, H_FOX))),
        "cache_odd_kv": nrm(ks[4], (n_pool, N_ODD, PAGE_SIZE, 2, 2 * NSA_KV_HEADS, HEAD_DIM)),
        "state_odd_win": nrm(ks[5], (DEC_BATCH, N_ODD, win_buf, 2, NSA_KV_HEADS, HEAD_DIM)),
        "page_table": jax.random.permutation(ks[6], n_pool)[:DEC_BATCH * n_pages].reshape(DEC_BATCH, n_pages).astype(jnp.int32),
        "rel_bias": nrm(ks[7], (N_BUCKETS, N_HEADS), 0.5),
        "ln_g": 1.0 + nrm(ks[8], (DEPTH, 2, d), 0.02),
        "ln_b": nrm(ks[9], (DEPTH, 2, d), 0.02),
        "w_in_even": nrm(ks[10], (N_EVEN, d, EVEN_COLS), d ** -0.5),
        "b_forget": FORGET_BIAS + nrm(ks[11], (N_EVEN, H_FOX), 0.1),
        "w_out_even": nrm(ks[12], (N_EVEN, d, d), BETA * d ** -0.5),
        "ffn_w13": nrm(ks[13], (N_EVEN, d, 2 * D_FF), d ** -0.5),
        "ffn_w2": nrm(ks[14], (N_EVEN, D_FF, d), BETA * D_FF ** -0.5),
        "w_in_odd": nrm(ks[15], (N_ODD, d, ODD_COLS), d ** -0.5),
        "b_gate": nrm(ks[16], (N_ODD, 3 * N_HEADS), 0.02),
        "cmp_pe": nrm(ks[17], (N_ODD, 2, CMP_LEN, HEAD_DIM), 0.02),
        "cmp_w1": nrm(ks[18], (N_ODD, 2, CMP_LEN, HEAD_DIM, HEAD_DIM), (CMP_LEN * HEAD_DIM) ** -0.5),
        "cmp_w2": nrm(ks[19], (N_ODD, 2, HEAD_DIM, HEAD_DIM), HEAD_DIM ** -0.5),
        "w_out_odd": nrm(ks[20], (N_ODD, d, d), BETA * d ** -0.5),
        "w_router": nrm(ks[21], (N_ODD, d, N_EXPERTS), d ** -0.5),
        "moe_w13": nrm(ks[22], (N_ODD, N_EXPERTS, d, 2 * D_FF), d ** -0.5),
        "moe_w2": nrm(ks[23], (N_ODD, N_EXPERTS, D_FF, d), BETA * D_FF ** -0.5),
    }


def reference(x_prompt, x_sample, cache_even_kv, cache_even_logf, cache_odd_kv, state_odd_win, page_table,
              rel_bias, ln_g, ln_b, w_in_even, b_forget, w_out_even, ffn_w13, ffn_w2,
              w_in_odd, b_gate, cmp_pe, cmp_w1, cmp_w2, w_out_odd, w_router, moe_w13, moe_w2):
    rel_tab = rel_bias.T
    dec_b, n_pages = page_table.shape
    past_len = n_pages * cache_even_kv.shape[2]

    def gather(cache, j):
        g = cache[page_table, j]
        return g.reshape((dec_b, past_len) + g.shape[3:])

    def run(x, pos0, sample):
        ekv, elf, okv, owin = [], [], [], []
        for layer in range(DEPTH):
            j = layer // 2
            if layer % 2 == 0:
                pk = gather(cache_even_kv, j) if sample else None
                pf = gather(cache_even_logf, j) if sample else None
                h, kv_new, lf_new = even_mixer(x, pos0, pk, pf, w_in_even[j], b_forget[j], w_out_even[j], rel_tab)
                ekv.append(kv_new)
                elf.append(lf_new)
                x = layer_norm(ALPHA * x + h, ln_g[layer, 0], ln_b[layer, 0])
                x = layer_norm(ALPHA * x + swiglu(x, ffn_w13[j], ffn_w2[j]), ln_g[layer, 1], ln_b[layer, 1])
            else:
                pk = gather(cache_odd_kv, j) if sample else None
                wb = state_odd_win[:, j] if sample else None
                h, kv_new, win_new = odd_mixer(x, pos0, pk, wb, w_in_odd[j], b_gate[j], cmp_pe[j], cmp_w1[j],
                                               cmp_w2[j], w_out_odd[j], rel_tab)
                okv.append(kv_new)
                owin.append(win_new)
                x = layer_norm(ALPHA * x + h, ln_g[layer, 0], ln_b[layer, 0])
                x = layer_norm(ALPHA * x + moe_swiglu(x, w_router[j], moe_w13[j], moe_w2[j]),
                               ln_g[layer, 1], ln_b[layer, 1])
        return (x, jnp.stack(ekv, 1), jnp.stack(elf, 1), jnp.stack(okv, 1), jnp.stack(owin, 1))

    y_prompt, p_even_kv, p_even_logf, p_odd_kv, p_win = run(x_prompt, 0, False)
    y_sample, s_even_kv, s_even_logf, s_odd_kv, s_win = run(x_sample, past_len, True)
    return (y_prompt, y_sample, p_even_kv, s_even_kv, p_even_logf, s_even_logf, p_odd_kv, s_odd_kv, p_win, s_win)
```

```python
import functools
import math

import jax
import jax.numpy as jnp
from jax import lax
from jax.experimental import pallas as pl
from jax.experimental.pallas import tpu as pltpu

F32 = jnp.float32
BF16 = jnp.bfloat16
HIGHEST = lax.Precision.HIGHEST

D_MODEL = 1024
DEPTH = 4
HEAD_DIM = 64
N_HEADS = 16
H_MOBA = 8
H_FOX = 8
MOBA_BLOCK = 256
MOBA_TOPK = 3
NSA_KV_HEADS = 4
NSA_GROUP = 4
CMP_LEN = 32
CMP_STRIDE = 16
SEL_BLOCK = 64
SEL_TOPN = 16
WINDOW = 512
N_BUCKETS = 32
MAX_DISTANCE = 4096
D_FF = 2816
N_EXPERTS = 8
PAGE_SIZE = 128
ALPHA = (2 * DEPTH) ** 0.25
NEG = -1e30
REMOVED = -3e38
FORCED_SCORE = 1e6
LN_EPS = 1e-5
SCALE = HEAD_DIM ** -0.5

LANES = 128
SUBLANES = 8
VMEM_LIMIT = 48 * 1024 * 1024


def _cparams(sem):
    return pltpu.CompilerParams(dimension_semantics=sem, vmem_limit_bytes=VMEM_LIMIT)


def _mm_kernel(x_ref, w_ref, b_ref, o_ref, *, act):
    y = jnp.dot(x_ref[...].astype(BF16), w_ref[...], preferred_element_type=F32)
    if act == "log_sigmoid":
        y = jax.nn.log_sigmoid(y + b_ref[...])
    elif act == "sigmoid":
        y = jax.nn.sigmoid(y + b_ref[...])
    o_ref[...] = y


def _mm(x, w, bias=None, act=None, tn=None):
    M, K = x.shape
    N = w.shape[1]
    tm = min(M, 512)
    tn = tn or min(N, 1024)
    if bias is None:
        bias = jnp.zeros((1, N), F32)
    return pl.pallas_call(
        functools.partial(_mm_kernel, act=act),
        out_shape=jax.ShapeDtypeStruct((M, N), F32),
        grid=(N // tn, M // tm),
        in_specs=[pl.BlockSpec((tm, K), lambda n, m: (m, 0)),
                  pl.BlockSpec((K, tn), lambda n, m: (0, n)),
                  pl.BlockSpec((1, tn), lambda n, m: (0, n))],
        out_specs=pl.BlockSpec((tm, tn), lambda n, m: (m, n)),
        compiler_params=_cparams(("parallel", "parallel")),
    )(x, w, bias)


def _layer_norm(z, g, b):
    mu = jnp.mean(z, axis=-1, keepdims=True)
    d = z - mu
    var = jnp.mean(d * d, axis=-1, keepdims=True)
    return d * lax.rsqrt(var + LN_EPS) * g + b


def _mm_ln_kernel(*refs, n_br):
    o_refs = refs[:n_br]
    gate_ref, w_ref, x_ref, g_ref, b_ref, y_ref = refs[n_br:]
    if n_br == 1:
        o = o_refs[0][...]
    else:
        gates = gate_ref[...]
        parts = []
        for h in range(N_HEADS):
            acc = None
            for br in range(n_br):
                c = h * n_br + br
                t = gates[:, c:c + 1] * o_refs[br][:, h * HEAD_DIM:(h + 1) * HEAD_DIM]
                acc = t if acc is None else acc + t
            parts.append(acc)
        o = jnp.concatenate(parts, axis=1)
    h = jnp.dot(o.astype(BF16), w_ref[...], preferred_element_type=F32)
    y_ref[...] = _layer_norm(ALPHA * x_ref[...] + h, g_ref[...], b_ref[...])


def _mm_ln(o_list, gates, w, x, g, b):
    M = x.shape[0]
    n_br = len(o_list)
    tm = min(M, 256)
    if gates is None:
        gates = jnp.zeros((M, LANES), F32)
    row = lambda m: (m, 0)
    fixed = lambda m: (0, 0)
    return pl.pallas_call(
        functools.partial(_mm_ln_kernel, n_br=n_br),
        out_shape=jax.ShapeDtypeStruct((M, D_MODEL), F32),
        grid=(M // tm,),
        in_specs=[pl.BlockSpec((tm, D_MODEL), row)] * n_br + [
            pl.BlockSpec((tm, LANES), row),
            pl.BlockSpec((D_MODEL, D_MODEL), fixed),
            pl.BlockSpec((tm, D_MODEL), row),
            pl.BlockSpec((1, D_MODEL), fixed),
            pl.BlockSpec((1, D_MODEL), fixed)],
        out_specs=pl.BlockSpec((tm, D_MODEL), row),
        compiler_params=_cparams(("parallel",)),
    )(*o_list, gates, w, x, g.reshape(1, -1), b.reshape(1, -1))


def _ffn_kernel(x_ref, gate_ref, wa_ref, wb_ref, w2_ref, g_ref, b_ref, y_ref, xb_ref, acc_ref, *, gated):
    e = pl.program_id(1)
    f = pl.program_id(2)

    @pl.when((e == 0) & (f == 0))
    def _():
        xb_ref[...] = x_ref[...].astype(BF16)
        acc_ref[...] = jnp.zeros_like(acc_ref)

    xb = xb_ref[...]
    a = jnp.dot(xb, wa_ref[0], preferred_element_type=F32)
    bb = jnp.dot(xb, wb_ref[0], preferred_element_type=F32)
    hh = jax.nn.silu(a) * bb
    part = jnp.dot(hh.astype(BF16), w2_ref[0], preferred_element_type=F32)
    if gated:
        gates = gate_ref[...]
        lane = lax.broadcasted_iota(jnp.int32, gates.shape, 1)
        part = part * jnp.sum(jnp.where(lane == e, gates, 0.0), axis=1, keepdims=True)
    acc_ref[...] += part

    @pl.when((e == pl.num_programs(1) - 1) & (f == pl.num_programs(2) - 1))
    def _():
        y_ref[...] = _layer_norm(ALPHA * x_ref[...] + acc_ref[...], g_ref[...], b_ref[...])


def _ffn(x, gates, w13, w2, g, b):
    M = x.shape[0]
    E = w13.shape[0]
    tm = min(M, 1024)
    tf = 256
    nf = D_FF // tf
    gated = gates is not None
    if gates is None:
        gates = jnp.zeros((M, LANES), F32)
    return pl.pallas_call(
        functools.partial(_ffn_kernel, gated=gated),
        out_shape=jax.ShapeDtypeStruct((M, D_MODEL), F32),
        grid=(M // tm, E, nf),
        in_specs=[pl.BlockSpec((tm, D_MODEL), lambda m, e, f: (m, 0)),
                  pl.BlockSpec((tm, LANES), lambda m, e, f: (m, 0)),
                  pl.BlockSpec((1, D_MODEL, tf), lambda m, e, f: (e, 0, f)),
                  pl.BlockSpec((1, D_MODEL, tf), lambda m, e, f: (e, 0, nf + f)),
                  pl.BlockSpec((1, tf, D_MODEL), lambda m, e, f: (e, f, 0)),
                  pl.BlockSpec((1, D_MODEL), lambda m, e, f: (0, 0)),
                  pl.BlockSpec((1, D_MODEL), lambda m, e, f: (0, 0))],
        out_specs=pl.BlockSpec((tm, D_MODEL), lambda m, e, f: (m, 0)),
        scratch_shapes=[pltpu.VMEM((tm, D_MODEL), BF16), pltpu.VMEM((tm, D_MODEL), F32)],
        compiler_params=_cparams(("parallel", "arbitrary", "arbitrary")),
    )(x, gates, w13, w13, w2, g.reshape(1, -1), b.reshape(1, -1))


def _router_kernel(x_ref, w_ref, o_ref):
    logits = jnp.dot(x_ref[...].astype(BF16), w_ref[...].astype(BF16), preferred_element_type=F32)
    lane = lax.broadcasted_iota(jnp.int32, logits.shape, 1).astype(F32)
    s = jnp.where(lane < N_EXPERTS, logits, REMOVED)
    m1 = jnp.max(s, axis=1, keepdims=True)
    i1 = jnp.min(jnp.where(s == m1, lane, 1e9), axis=1, keepdims=True)
    s2 = jnp.where(lane == i1, REMOVED, s)
    m2 = jnp.max(s2, axis=1, keepdims=True)
    i2 = jnp.min(jnp.where(s2 == m2, lane, 1e9), axis=1, keepdims=True)
    e2 = jnp.exp(m2 - m1)
    den = 1.0 + e2
    o_ref[...] = jnp.where(lane == i1, 1.0 / den, 0.0) + jnp.where(lane == i2, e2 / den, 0.0)


def _router(x, w_router):
    M = x.shape[0]
    tm = min(M, 512)
    w = jnp.pad(w_router, ((0, 0), (0, LANES - N_EXPERTS)))
    return pl.pallas_call(
        _router_kernel,
        out_shape=jax.ShapeDtypeStruct((M, LANES), F32),
        grid=(M // tm,),
        in_specs=[pl.BlockSpec((tm, D_MODEL), lambda m: (m, 0)),
                  pl.BlockSpec((D_MODEL, LANES), lambda m: (0, 0))],
        out_specs=pl.BlockSpec((tm, LANES), lambda m: (m, 0)),
        compiler_params=_cparams(("parallel",)),
    )(x, w)


def _kmean_kernel(k_ref, o_ref, *, nb):
    k = k_ref[0, 0].reshape(nb, MOBA_BLOCK, HEAD_DIM)
    o_ref[0, 0] = jnp.sum(k, axis=1) * (1.0 / MOBA_BLOCK)


def _kmean(k, n_heads):
    B, _, L, _ = k.shape
    nb = L // MOBA_BLOCK
    km = pl.pallas_call(
        functools.partial(_kmean_kernel, nb=nb),
        out_shape=jax.ShapeDtypeStruct((B, n_heads, nb, HEAD_DIM), F32),
        grid=(B, n_heads),
        in_specs=[pl.BlockSpec((1, 1, L, HEAD_DIM), lambda b, h: (b, h, 0, 0))],
        out_specs=pl.BlockSpec((1, 1, nb, HEAD_DIM), lambda b, h: (b, h, 0, 0)),
        compiler_params=_cparams(("parallel", "parallel")),
    )(k)
    return jnp.pad(km, ((0, 0), (0, 0), (0, LANES - nb), (0, 0)))


def _cumsum_kernel(x_ref, o_ref, carry_ref, *, chunk):
    j = pl.program_id(1)

    @pl.when(j == 0)
    def _():
        carry_ref[...] = jnp.zeros_like(carry_ref)

    r = lax.broadcasted_iota(jnp.int32, (chunk, chunk), 0)
    c = lax.broadcasted_iota(jnp.int32, (chunk, chunk), 1)
    tri = (r <= c).astype(F32)
    cs = jnp.dot(x_ref[0], tri, preferred_element_type=F32, precision=HIGHEST) + carry_ref[...]
    o_ref[0] = cs
    carry_ref[...] = cs[:, chunk - 1:chunk]


def _cumsum_lanes(x):
    B, H, L = x.shape
    chunk = 256
    return pl.pallas_call(
        functools.partial(_cumsum_kernel, chunk=chunk),
        out_shape=jax.ShapeDtypeStruct((B, H, L), F32),
        grid=(B, L // chunk),
        in_specs=[pl.BlockSpec((1, H, chunk), lambda b, j: (b, 0, j))],
        out_specs=pl.BlockSpec((1, H, chunk), lambda b, j: (b, 0, j)),
        scratch_shapes=[pltpu.VMEM((H, 1), F32)],
        compiler_params=_cparams(("parallel", "arbitrary")),
    )(x)


def _topk_mask(score, k, lane_f):
    sel = jnp.zeros_like(score)
    for _ in range(k):
        mx = jnp.max(score, axis=1, keepdims=True)
        first = jnp.min(jnp.where(score == mx, lane_f, 1e9), axis=1, keepdims=True)
        pick = lane_f == first
        sel = jnp.where(pick, 1.0, sel)
        score = jnp.where(pick, REMOVED, score)
    return sel


def _toeplitz(rv_ref, h, off, tq, tk):
    w = tq + tk
    vec = rv_ref[h, :, pl.ds(pl.multiple_of(off, LANES), w)]
    big = jnp.broadcast_to(vec, (tq, w))
    return pltpu.roll(big, tk + 1, 1, stride=1, stride_axis=0)[:, :tk]


def _flash_kernel(*refs, mode, bias_kind, hb, g, tq, tk, qbase, kbase, dmax, ktile, kactive, klast, two_pass, nk):
    it = iter(refs)
    q_ref, k_ref, v_ref = next(it), next(it), next(it)
    bias_ref = next(it) if bias_kind else None
    aux_ref = next(it) if mode in ("moba", "fox", "sel") else None
    cq_ref = next(it) if mode == "fox" else None
    o_ref, m_ref, l_ref, acc_ref = next(it), next(it), next(it), next(it)
    sel_ref = next(it) if mode == "moba" else None

    qi = pl.program_id(2)
    kk = pl.program_id(3)
    kt = ktile(qi, kk)
    q0 = qbase + qi * tq
    k0 = kbase + kt * tk

    def online_update(h, s, valid, vb16):
        m_prev = m_ref[h]
        m_new = jnp.maximum(m_prev, jnp.max(s, axis=1, keepdims=True))
        p = jnp.where(valid, jnp.exp(s - m_new), 0.0)
        alpha = jnp.exp(m_prev - m_new)
        l_ref[h] = alpha * l_ref[h] + jnp.sum(p, axis=1, keepdims=True)
        if not two_pass:
            acc_ref[h] = alpha * acc_ref[h] + jnp.dot(p.astype(BF16), vb16, preferred_element_type=F32)
        m_ref[h] = m_new

    def normalized_update(h, s, valid, vb16):
        p = jnp.where(valid, jnp.exp(s - m_ref[h]), 0.0) / jnp.maximum(l_ref[h], 1e-30)
        acc_ref[h] = acc_ref[h] + jnp.dot(p.astype(BF16), vb16, preferred_element_type=F32)

    @pl.when(kk == 0)
    def _init():
        m_ref[...] = jnp.full(m_ref.shape, NEG, F32)
        l_ref[...] = jnp.zeros_like(l_ref)
        acc_ref[...] = jnp.zeros_like(acc_ref)
        if mode == "moba":
            nb = aux_ref.shape[2]
            blk = lax.broadcasted_iota(jnp.int32, (tq, nb), 1)
            own = (q0 + lax.broadcasted_iota(jnp.int32, (tq, nb), 0)) >> 8
            ok = blk < own
            blk_f = blk.astype(F32)
            for h in range(hb):
                gsc = lax.dot_general(q_ref[0, h].astype(BF16), aux_ref[0, h].astype(BF16),
                                      (((1,), (1,)), ((), ())), preferred_element_type=F32)
                picked = _topk_mask(jnp.where(ok, gsc, NEG), MOBA_TOPK, blk_f)
                sel_ref[h] = jnp.where(ok, picked, 0.0)

    def step(update):
        qpos = q0 + lax.broadcasted_iota(jnp.int32, (tq, tk), 0)
        kpos = k0 + lax.broadcasted_iota(jnp.int32, (tq, tk), 1)
        dist = qpos - kpos
        if mode == "win":
            base_ok = (dist >= 0) & (dist < WINDOW)
        else:
            base_ok = dist >= 0
        if mode == "sel":
            flags = aux_ref[0, 0].astype(BF16)
            nbf = flags.shape[1]
            blk = lax.broadcasted_iota(jnp.int32, (nbf, tk), 0)
            kb = (k0 + lax.broadcasted_iota(jnp.int32, (nbf, tk), 1)) >> 6
            expand = (blk == kb).astype(BF16)
            picked = jnp.dot(flags, expand, preferred_element_type=F32)
            base_ok = base_ok & (picked > 0.5)
        if mode == "moba":
            own_f = jnp.where(((q0 + lax.broadcasted_iota(jnp.int32, (tq, 1), 0)) >> 8) == (k0 >> 8), 1.0, 0.0)
        for hk in range(hb // g):
            kb16 = k_ref[0, hk].astype(BF16)
            vb16 = v_ref[0, hk].astype(BF16)
            for gi in range(g):
                h = hk * g + gi
                qs = (q_ref[0, h] * SCALE).astype(BF16)
                s = lax.dot_general(qs, kb16, (((1,), (1,)), ((), ())), preferred_element_type=F32)
                if bias_kind == "toeplitz":
                    s = s + _toeplitz(bias_ref, h, dmax - (q0 - k0), tq, tk)
                elif bias_kind == "table":
                    s = s + bias_ref[h]
                if mode == "fox":
                    s = (s + cq_ref[0, h]) - aux_ref[0, h]
                if mode == "moba":
                    sel = sel_ref[h]
                    lane = lax.broadcasted_iota(jnp.int32, sel.shape, 1)
                    selcol = jnp.sum(jnp.where(lane == (k0 >> 8), sel, 0.0), axis=1, keepdims=True)
                    valid = (own_f * jnp.where(base_ok, 1.0, 0.0) + (1.0 - own_f) * selcol) > 0.5
                else:
                    valid = base_ok
                update(h, jnp.where(valid, s, NEG), valid, vb16)

    if two_pass:
        @pl.when(kk < nk)
        def _stats():
            step(online_update)

        @pl.when(kk >= nk)
        def _weighted():
            step(normalized_update)

        @pl.when(kk == 2 * nk - 1)
        def _fin2():
            for h in range(hb):
                o_ref[0, h] = acc_ref[h]
    else:
        @pl.when(kactive(qi, kk))
        def _step():
            step(online_update)

        @pl.when(kk == klast(qi))
        def _fin():
            for h in range(hb):
                o_ref[0, h] = acc_ref[h] / jnp.maximum(l_ref[h], 1e-30)


def _flash(mode, q, k, v, *, tq, tk, hb, g=1, qbase=0, kbase=0, bias=None, bias_kind=None, aux=None, cq=None,
           two_pass=False):
    B, H, Tq, _ = q.shape
    Lk = k.shape[2]
    nq = Tq // tq
    hkb = hb // g
    dmax = 0
    if mode == "win" and Lk > tk:
        nk = WINDOW // tk + tq // tk
        ktile = lambda qi, kk: jnp.maximum(qi * (tq // tk) - WINDOW // tk + kk, 0)
        kactive = lambda qi, kk: qi * (tq // tk) - WINDOW // tk + kk >= 0
        klast = lambda qi: nk - 1
    elif mode == "win":
        nk = 1
        ktile = lambda qi, kk: kk
        kactive = lambda qi, kk: kk >= 0
        klast = lambda qi: 0
    else:
        nk = Lk // tk
        last = lambda qi: jnp.minimum((qbase + qi * tq + tq - 1 - kbase) // tk, nk - 1)
        ktile = lambda qi, kk: jnp.minimum(kk, last(qi))
        kactive = lambda qi, kk: kk <= last(qi)
        klast = last
    if bias_kind == "toeplitz":
        dmax = bias.shape[2] - (tq + tk)
    vtile = ktile
    nsteps = nk
    if two_pass:
        assert nq == 1 and (mode == "win" or (qbase + tq - 1 - kbase) // tk >= nk - 1)
        ktile = lambda qi, kk: kk % nk
        vtile = lambda qi, kk: jnp.maximum(kk - nk, 0)
        nsteps = 2 * nk

    kv_spec = pl.BlockSpec((1, hkb, tk, HEAD_DIM), lambda b, hg, qi, kk: (b, hg, ktile(qi, kk), 0))
    v_spec = pl.BlockSpec((1, hkb, tk, HEAD_DIM), lambda b, hg, qi, kk: (b, hg, vtile(qi, kk), 0))
    in_specs = [pl.BlockSpec((1, hb, tq, HEAD_DIM), lambda b, hg, qi, kk: (b, hg, qi, 0)), kv_spec, v_spec]
    args = [q, k, v]
    if bias_kind == "toeplitz":
        in_specs.append(pl.BlockSpec((hb, 1, bias.shape[2]), lambda b, hg, qi, kk: (hg, 0, 0)))
        args.append(bias)
    elif bias_kind == "table":
        in_specs.append(pl.BlockSpec((hb, tq, tk), lambda b, hg, qi, kk: (hg, qi, ktile(qi, kk))))
        args.append(bias)
    scratch = [pltpu.VMEM((hb, tq, 1), F32), pltpu.VMEM((hb, tq, 1), F32), pltpu.VMEM((hb, tq, HEAD_DIM), F32)]
    if mode == "moba":
        nb = aux.shape[2]
        in_specs.append(pl.BlockSpec((1, hb, nb, HEAD_DIM), lambda b, hg, qi, kk: (b, hg, 0, 0)))
        args.append(aux)
        scratch.append(pltpu.VMEM((hb, tq, nb), F32))
    elif mode == "fox":
        in_specs.append(pl.BlockSpec((1, hb, 1, tk), lambda b, hg, qi, kk: (b, hg, 0, ktile(qi, kk))))
        in_specs.append(pl.BlockSpec((1, hb, tq, 1), lambda b, hg, qi, kk: (b, hg, qi, 0)))
        args += [aux, cq]
    elif mode == "sel":
        in_specs.append(pl.BlockSpec((1, 1, tq, aux.shape[3]), lambda b, hg, qi, kk: (b, hg, qi, 0)))
        args.append(aux)
    return pl.pallas_call(
        functools.partial(_flash_kernel, mode=mode, bias_kind=bias_kind, hb=hb, g=g, tq=tq, tk=tk,
                          qbase=qbase, kbase=kbase, dmax=dmax, ktile=ktile, kactive=kactive, klast=klast,
                          two_pass=two_pass, nk=nk),
        out_shape=jax.ShapeDtypeStruct((B, H, Tq, HEAD_DIM), F32),
        grid=(B, H // hb, nq, nsteps),
        in_specs=in_specs,
        out_specs=pl.BlockSpec((1, hb, tq, HEAD_DIM), lambda b, hg, qi, kk: (b, hg, qi, 0)),
        scratch_shapes=scratch,
        compiler_params=_cparams(("parallel", "parallel", "parallel", "arbitrary")),
    )(*args)


def _compress_kernel(x_ref, pe_ref, wab_ref, w2_ref, o_ref):
    nch = x_ref.shape[2]
    half = CMP_STRIDE * HEAD_DIM
    x = x_ref[0, 0]
    wab = wab_ref[...].astype(BF16)
    first = jnp.dot((x + pe_ref[:, :half]).astype(BF16), wab[:, :HEAD_DIM], preferred_element_type=F32)
    second = jnp.dot((x + pe_ref[:, half:]).astype(BF16), wab[:, HEAD_DIM:], preferred_element_type=F32)
    hmid = first + pltpu.roll(second, nch - 1, 0)
    o_ref[0, 0] = jnp.dot(jax.nn.silu(hmid).astype(BF16), w2_ref[...].astype(BF16), preferred_element_type=F32)


def _compress(x, slot0, nch, pe, w1, w2):
    B = x.shape[0]
    half = CMP_STRIDE * HEAD_DIM
    wab = jnp.concatenate([w1[:CMP_STRIDE].reshape(half, HEAD_DIM), w1[CMP_STRIDE:].reshape(half, HEAD_DIM)], axis=1)
    return pl.pallas_call(
        _compress_kernel,
        out_shape=jax.ShapeDtypeStruct((B, NSA_KV_HEADS, nch, HEAD_DIM), F32),
        grid=(B, NSA_KV_HEADS),
        in_specs=[pl.BlockSpec((1, 1, nch, half), lambda b, h: (b, slot0 + h, 0, 0)),
                  pl.BlockSpec((1, 2 * half), lambda b, h: (0, 0)),
                  pl.BlockSpec((half, 2 * HEAD_DIM), lambda b, h: (0, 0)),
                  pl.BlockSpec((HEAD_DIM, HEAD_DIM), lambda b, h: (0, 0))],
        out_specs=pl.BlockSpec((1, 1, nch, HEAD_DIM), lambda b, h: (b, h, 0, 0)),
        compiler_params=_cparams(("parallel", "parallel")),
    )(x, pe.reshape(1, 2 * half), wab, w2)


def _split3(x):
    hi = x.astype(BF16)
    r1 = x - hi.astype(F32)
    mid = r1.astype(BF16)
    lo = (r1 - mid.astype(F32)).astype(BF16)
    return hi, mid, lo


def _cmp_kernel(q_ref, kc_ref, vc_ref, o_ref, f_ref, *, tq, qbase, nbf):
    qi = pl.program_id(2)
    q0 = qbase + qi * tq
    nc1 = kc_ref.shape[2]
    kc = kc_ref[0, 0].astype(BF16)
    vc = vc_ref[0, 0].astype(BF16)
    qpos = q0 + lax.broadcasted_iota(jnp.int32, (tq, nc1), 0)
    n = lax.broadcasted_iota(jnp.int32, (tq, nc1), 1)
    valid = (n * CMP_STRIDE + (CMP_LEN - 1) <= qpos) & (n < nc1 - 1)
    psum = jnp.zeros((tq, nc1), F32)
    for gi in range(NSA_GROUP):
        qs = (q_ref[0, gi] * SCALE).astype(BF16)
        s = lax.dot_general(qs, kc, (((1,), (1,)), ((), ())), preferred_element_type=F32)
        s = jnp.where(valid, s, NEG)
        m = jnp.max(s, axis=1, keepdims=True)
        p = jnp.where(valid, jnp.exp(s - m), 0.0)
        p = p / jnp.maximum(jnp.sum(p, axis=1, keepdims=True), 1e-30)
        o_ref[0, gi] = jnp.dot(p.astype(BF16), vc, preferred_element_type=F32)
        psum = psum + p
    rn = lax.broadcasted_iota(jnp.int32, (nc1, nbf), 0)
    cj = lax.broadcasted_iota(jnp.int32, (nc1, nbf), 1)
    r = SEL_BLOCK // CMP_STRIDE
    ov = ((rn >= r * cj - 1) & (rn <= r * cj + r - 1) & (rn < nc1 - 1)).astype(BF16)
    pslc = jnp.zeros((tq, nbf), F32)
    for part in _split3(psum):
        pslc = pslc + jnp.dot(part, ov, preferred_element_type=F32)
    blk = lax.broadcasted_iota(jnp.int32, (tq, nbf), 1)
    own = (q0 + lax.broadcasted_iota(jnp.int32, (tq, nbf), 0)) >> 6
    ok = blk <= own
    forced = (blk == 0) | (blk == own) | (blk == own - 1)
    score = jnp.where(ok, jnp.where(forced, FORCED_SCORE, pslc), NEG)
    picked = _topk_mask(score, SEL_TOPN, blk.astype(F32))
    f_ref[0, 0] = jnp.where(ok, picked, 0.0)


def _cmp_attention(q, kc, vc, *, tq, qbase, nbf):
    B, H, Tq, _ = q.shape
    nc1 = kc.shape[2]
    kv_spec = pl.BlockSpec((1, 1, nc1, HEAD_DIM), lambda b, h, qi: (b, h, 0, 0))
    return pl.pallas_call(
        functools.partial(_cmp_kernel, tq=tq, qbase=qbase, nbf=nbf),
        out_shape=(jax.ShapeDtypeStruct((B, H, Tq, HEAD_DIM), F32),
                   jax.ShapeDtypeStruct((B, NSA_KV_HEADS, Tq, nbf), F32)),
        grid=(B, NSA_KV_HEADS, Tq // tq),
        in_specs=[pl.BlockSpec((1, NSA_GROUP, tq, HEAD_DIM), lambda b, h, qi: (b, h, qi, 0)), kv_spec, kv_spec],
        out_specs=(pl.BlockSpec((1, NSA_GROUP, tq, HEAD_DIM), lambda b, h, qi: (b, h, qi, 0)),
                   pl.BlockSpec((1, 1, tq, nbf), lambda b, h, qi: (b, h, qi, 0))),
        compiler_params=_cparams(("parallel", "parallel", "parallel")),
    )(q, kc, vc)


def _gather_kernel(pt_ref, cache_ref, new_ref, o_ref, *, n_pages, n_slots):
    p = pl.program_id(1)

    @pl.when(p < n_pages)
    def _():
        for s in range(n_slots):
            o_ref[0, s] = cache_ref[0, 0, :, s * HEAD_DIM:(s + 1) * HEAD_DIM]

    @pl.when(p >= n_pages)
    def _():
        for s in range(n_slots):
            o_ref[0, s] = new_ref[0, :, s * HEAD_DIM:(s + 1) * HEAD_DIM]


def _gather_pages(cache, page_table, layer, new_rows):
    B, n_pages = page_table.shape
    C = cache.shape[3]
    n_slots = C // HEAD_DIM
    extra = new_rows.shape[1] // PAGE_SIZE
    L = (n_pages + extra) * PAGE_SIZE
    grid_spec = pltpu.PrefetchScalarGridSpec(
        num_scalar_prefetch=1,
        grid=(B, n_pages + extra),
        in_specs=[pl.BlockSpec((1, 1, PAGE_SIZE, C),
                               lambda b, p, pt: (pt[b, jnp.minimum(p, n_pages - 1)], layer, 0, 0)),
                  pl.BlockSpec((1, PAGE_SIZE, C), lambda b, p, pt: (b, jnp.maximum(p - n_pages, 0), 0))],
        out_specs=pl.BlockSpec((1, n_slots, PAGE_SIZE, HEAD_DIM), lambda b, p, pt: (b, 0, p, 0)),
    )
    return pl.pallas_call(
        functools.partial(_gather_kernel, n_pages=n_pages, n_slots=n_slots),
        out_shape=jax.ShapeDtypeStruct((B, n_slots, L, HEAD_DIM), F32),
        grid_spec=grid_spec,
        compiler_params=_cparams(("parallel", "arbitrary")),
    )(page_table, cache, new_rows)


PAGES_PER_STEP = 8


def _gather_flat_kernel(pt_ref, *refs):
    o_ref = refs[-1]
    for i, c_ref in enumerate(refs[:-1]):
        o_ref[0, i] = c_ref[0, 0]


def _gather_logf(cache, page_table, layer):
    B, n_pages = page_table.shape
    n_pool, nl, ps, w = cache.shape
    flat = cache.reshape(n_pool, nl, 1, ps * w)
    spec = lambda i: pl.BlockSpec((1, 1, 1, ps * w),
                                  lambda b, s, pt: (pt[b, s * PAGES_PER_STEP + i], layer, 0, 0))
    grid_spec = pltpu.PrefetchScalarGridSpec(
        num_scalar_prefetch=1,
        grid=(B, n_pages // PAGES_PER_STEP),
        in_specs=[spec(i) for i in range(PAGES_PER_STEP)],
        out_specs=pl.BlockSpec((1, PAGES_PER_STEP, 1, ps * w), lambda b, s, pt: (b, s, 0, 0)),
    )
    out = pl.pallas_call(
        _gather_flat_kernel,
        out_shape=jax.ShapeDtypeStruct((B, n_pages, 1, ps * w), F32),
        grid_spec=grid_spec,
        compiler_params=_cparams(("parallel", "arbitrary")),
    )(page_table, *([flat] * PAGES_PER_STEP))
    return out.reshape(B, n_pages * ps, w)


def _rel_bucket(dist):
    n = jnp.maximum(dist, 0)
    max_exact = N_BUCKETS // 2
    nf = jnp.maximum(n, 1).astype(F32)
    large = max_exact + (jnp.log(nf / max_exact) / math.log(MAX_DISTANCE / max_exact)
                         * (N_BUCKETS - max_exact)).astype(jnp.int32)
    large = jnp.minimum(large, N_BUCKETS - 1)
    return jnp.where(n < max_exact, n, large)


def _reversed_bias_table(rel_tab, dmax, tq, tk):
    length = dmax + tq + tk
    d = dmax + tq - 1 - jnp.arange(length)
    vals = jnp.where(d >= 0, rel_tab[:, _rel_bucket(d)], 0.0)
    return vals[:, None, :].astype(F32)


def _decode_bias_table(rel_tab, qbase, kbase, tq, lk):
    d = (qbase + jnp.arange(tq))[:, None] - (kbase + jnp.arange(lk))[None, :]
    return jnp.where(d >= 0, rel_tab[:, _rel_bucket(d)], 0.0).astype(F32)


def _to_heads(x, n):
    B, T, _ = x.shape
    return x.reshape(B, T, n, HEAD_DIM).transpose(0, 2, 1, 3)


def _from_heads(x):
    B, H, T, _ = x.shape
    return x.transpose(0, 2, 1, 3).reshape(B * T, H * HEAD_DIM)


def _pad_rows(x, rows):
    return jnp.pad(x, ((0, 0), (0, rows - x.shape[1])) + ((0, 0),) * (x.ndim - 2))


DEC_TQ = 8


def _even_layer(x, B, T, sample, j, p):
    nq = N_HEADS * HEAD_DIM
    w_in = p["w_in_even"][j]
    q = _mm(x, w_in[:, :nq].astype(BF16))
    kv = _mm(x, w_in[:, nq:3 * nq].astype(BF16))
    wf = jnp.pad(w_in[:, 3 * nq:], ((0, 0), (0, LANES - H_FOX))).astype(BF16)
    bf = jnp.pad(p["b_forget"][j], (0, LANES - H_FOX)).reshape(1, LANES)
    logf = _mm(x, wf, bias=bf, act="log_sigmoid")[:, :H_FOX]
    kv3 = kv.reshape(B, T, 2 * nq)
    logf3 = logf.reshape(B, T, H_FOX)
    rel_tab = p["rel_tab"]
    if not sample:
        qh = _to_heads(q.reshape(B, T, nq), N_HEADS)
        kvh = _to_heads(kv3, 2 * N_HEADS)
        kh, vh = kvh[:, :N_HEADS], kvh[:, N_HEADS:]
        kmean = _kmean(kh, H_MOBA)
        t = MOBA_BLOCK
        rv = _reversed_bias_table(rel_tab[:H_MOBA], T - t, t, t)
        o_a = _flash("moba", qh[:, :H_MOBA], kh[:, :H_MOBA], vh[:, :H_MOBA], tq=t, tk=t, hb=4,
                     bias=rv, bias_kind="toeplitz", aux=kmean)
        c = _cumsum_lanes(logf3.transpose(0, 2, 1)).reshape(B, H_FOX, 1, T)
        tf = min(512, T)
        o_b = _flash("fox", qh[:, H_MOBA:], kh[:, H_MOBA:], vh[:, H_MOBA:], tq=tf, tk=tf, hb=2, aux=c,
                     cq=c.reshape(B, H_FOX, T, 1))
    else:
        past = p["page_table"].shape[1] * PAGE_SIZE
        lk = past + MOBA_BLOCK
        new_rows = _pad_rows(kv3, MOBA_BLOCK)
        kvh = _gather_pages(p["cache_even_kv"], p["page_table"], j, new_rows)
        kh, vh = kvh[:, :N_HEADS], kvh[:, N_HEADS:]
        qh = _to_heads(_pad_rows(q.reshape(B, T, nq), DEC_TQ), N_HEADS)
        kmean = _kmean(kh, H_MOBA)
        tb = _decode_bias_table(rel_tab[:H_MOBA], past, 0, DEC_TQ, lk)
        o_a = _flash("moba", qh[:, :H_MOBA], kh[:, :H_MOBA], vh[:, :H_MOBA], tq=DEC_TQ, tk=MOBA_BLOCK, hb=8,
                     qbase=past, bias=tb, bias_kind="table", aux=kmean, two_pass=True)
        lf_past = _gather_logf(p["cache_even_logf"], p["page_table"], j)
        lf_all = jnp.concatenate([lf_past, _pad_rows(logf3, MOBA_BLOCK)], axis=1)
        c = _cumsum_lanes(lf_all.transpose(0, 2, 1)).reshape(B, H_FOX, 1, lk)
        tkf = lk // 3
        o_b = _flash("fox", qh[:, H_MOBA:], kh[:, H_MOBA:], vh[:, H_MOBA:], tq=DEC_TQ, tk=tkf, hb=2,
                     qbase=past, aux=c, cq=c[:, :, 0, past:past + DEC_TQ].reshape(B, H_FOX, DEC_TQ, 1),
                     two_pass=True)
        o_a, o_b = o_a[:, :, :T], o_b[:, :, :T]
    o = _from_heads(jnp.concatenate([o_a, o_b], axis=1))
    layer = 2 * j
    x1 = _mm_ln([o], None, p["w_out_even"][j].astype(BF16), x, p["ln_g"][layer, 0], p["ln_b"][layer, 0])
    x2 = _ffn(x1, None, p["ffn_w13"][j:j + 1].astype(BF16), p["ffn_w2"][j:j + 1].astype(BF16),
              p["ln_g"][layer, 1], p["ln_b"][layer, 1])
    return x2, kv3, logf3


def _odd_layer(x, B, T, sample, j, p):
    hk, d = NSA_KV_HEADS, HEAD_DIM
    nq = N_HEADS * d
    nkv = 4 * hk * d
    nw = 2 * hk * d
    w_in = p["w_in_odd"][j]
    q = _mm(x, w_in[:, :nq].astype(BF16))
    kv = _mm(x, w_in[:, nq:nq + nkv].astype(BF16))
    kvw = _mm(x, w_in[:, nq + nkv:nq + nkv + nw].astype(BF16), tn=nw)
    ngate = 3 * N_HEADS
    wg = jnp.pad(w_in[:, nq + nkv + nw:], ((0, 0), (0, LANES - ngate))).astype(BF16)
    bg = jnp.pad(p["b_gate"][j], (0, LANES - ngate)).reshape(1, LANES)
    gates = _mm(x, wg, bias=bg, act="sigmoid")
    kv3 = kv.reshape(B, T, nkv)
    kvw3 = kvw.reshape(B, T, nw)
    rel_tab = p["rel_tab"]
    pe, w1, w2 = p["cmp_pe"][j], p["cmp_w1"][j], p["cmp_w2"][j]
    if not sample:
        qh = _to_heads(q.reshape(B, T, nq), N_HEADS)
        kvh = _to_heads(kv3, 4 * hk)
        chunks = kvh.reshape(B, 4 * hk, T // CMP_STRIDE, CMP_STRIDE * d)
        kc = _compress(chunks, 0, T // CMP_STRIDE, pe[0], w1[0], w2[0])
        vc = _compress(chunks, 2 * hk, T // CMP_STRIDE, pe[1], w1[1], w2[1])
        t = 256
        o_cmp, flags = _cmp_attention(qh, kc, vc, tq=t, qbase=0, nbf=T // SEL_BLOCK)
        rv = _reversed_bias_table(rel_tab, T - t, t, t)
        o_sel = _flash("sel", qh, kvh[:, hk:2 * hk], kvh[:, 3 * hk:], tq=t, tk=t, hb=NSA_GROUP, g=NSA_GROUP,
                       bias=rv, bias_kind="toeplitz", aux=flags)
        kwh = _to_heads(kvw3, 2 * hk)
        o_win = _flash("win", qh, kwh[:, :hk], kwh[:, hk:], tq=t, tk=t, hb=NSA_GROUP, g=NSA_GROUP,
                       bias=rv, bias_kind="toeplitz")
    else:
        past = p["page_table"].shape[1] * PAGE_SIZE
        lk = past + 256
        new_rows = _pad_rows(kv3, 256)
        kvh = _gather_pages(p["cache_odd_kv"], p["page_table"], j, new_rows)
        qh = _to_heads(_pad_rows(q.reshape(B, T, nq), DEC_TQ), N_HEADS)
        chunks = kvh.reshape(B, 4 * hk, lk // CMP_STRIDE, CMP_STRIDE * d)
        nch = (past + T) // CMP_STRIDE
        kc = _compress(chunks, 0, nch, pe[0], w1[0], w2[0])
        vc = _compress(chunks, 2 * hk, nch, pe[1], w1[1], w2[1])
        nbf = 256
        o_cmp, flags = _cmp_attention(qh, kc, vc, tq=DEC_TQ, qbase=past, nbf=nbf)
        tb = _decode_bias_table(rel_tab, past, 0, DEC_TQ, lk)
        o_sel = _flash("sel", qh, kvh[:, hk:2 * hk], kvh[:, 3 * hk:], tq=DEC_TQ, tk=256, hb=NSA_GROUP,
                       g=NSA_GROUP, qbase=past, bias=tb, bias_kind="table", aux=flags, two_pass=True)
        win = p["state_odd_win"][:, j]
        wlen = win.shape[1]
        lw = wlen + LANES
        ext = jnp.concatenate([win.reshape(B, wlen, nw), kvw3], axis=1)
        kwh = _to_heads(_pad_rows(ext, lw), 2 * hk)
        tbw = _decode_bias_table(rel_tab, past, past - wlen, DEC_TQ, lw)
        o_win = _flash("win", qh, kwh[:, :hk], kwh[:, hk:], tq=DEC_TQ, tk=lw, hb=NSA_GROUP, g=NSA_GROUP,
                       qbase=past, kbase=past - wlen, bias=tbw, bias_kind="table", two_pass=True)
        o_cmp, o_sel, o_win = o_cmp[:, :, :T], o_sel[:, :, :T], o_win[:, :, :T]
    layer = 2 * j + 1
    x1 = _mm_ln([_from_heads(o_cmp), _from_heads(o_sel), _from_heads(o_win)], gates,
                p["w_out_odd"][j].astype(BF16), x, p["ln_g"][layer, 0], p["ln_b"][layer, 0])
    rg = _router(x1, p["w_router"][j])
    x2 = _ffn(x1, rg, p["moe_w13"][j].astype(BF16), p["moe_w2"][j].astype(BF16),
              p["ln_g"][layer, 1], p["ln_b"][layer, 1])
    return x2, kv3, kvw3


def _run(x3, sample, p):
    B, T, _ = x3.shape
    x = x3.reshape(B * T, D_MODEL)
    ekv, elf, okv, owin = [], [], [], []
    for layer in range(DEPTH):
        j = layer // 2
        if layer % 2 == 0:
            x, kv_new, lf_new = _even_layer(x, B, T, sample, j, p)
            ekv.append(kv_new.reshape(B, T, 2, N_HEADS, HEAD_DIM))
            elf.append(lf_new)
        else:
            x, kv_new, kvw_new = _odd_layer(x, B, T, sample, j, p)
            okv.append(kv_new.reshape(B, T, 2, 2 * NSA_KV_HEADS, HEAD_DIM))
            kvw5 = kvw_new.reshape(B, T, 2, NSA_KV_HEADS, HEAD_DIM)
            if sample:
                win = p["state_odd_win"][:, j]
                owin.append(jnp.concatenate([win, kvw5], axis=1)[:, -win.shape[1]:])
            else:
                owin.append(kvw5[:, -min(WINDOW, T):])
    return (x.reshape(B, T, D_MODEL), jnp.stack(ekv, 1), jnp.stack(elf, 1), jnp.stack(okv, 1), jnp.stack(owin, 1))


def kernel(x_prompt, x_sample, cache_even_kv, cache_even_logf, cache_odd_kv, state_odd_win, page_table, rel_bias, ln_g, ln_b, w_in_even, b_forget, w_out_even, ffn_w13, ffn_w2, w_in_odd, b_gate, cmp_pe, cmp_w1, cmp_w2, w_out_odd, w_router, moe_w13, moe_w2):
    n_pool = cache_even_kv.shape[0]
    p = dict(
        rel_tab=rel_bias.T,
        page_table=page_table,
        cache_even_kv=cache_even_kv.reshape(n_pool, cache_even_kv.shape[1], PAGE_SIZE, 2 * N_HEADS * HEAD_DIM),
        cache_even_logf=cache_even_logf,
        cache_odd_kv=cache_odd_kv.reshape(n_pool, cache_odd_kv.shape[1], PAGE_SIZE, 4 * NSA_KV_HEADS * HEAD_DIM),
        state_odd_win=state_odd_win,
        ln_g=ln_g, ln_b=ln_b, w_in_even=w_in_even, b_forget=b_forget, w_out_even=w_out_even,
        ffn_w13=ffn_w13, ffn_w2=ffn_w2, w_in_odd=w_in_odd, b_gate=b_gate, cmp_pe=cmp_pe, cmp_w1=cmp_w1,
        cmp_w2=cmp_w2, w_out_odd=w_out_odd, w_router=w_router, moe_w13=moe_w13, moe_w2=moe_w2)
    y_p, p_ekv, p_elf, p_okv, p_win = _run(x_prompt, False, p)
    y_s, s_ekv, s_elf, s_okv, s_win = _run(x_sample, True, p)
    return (y_p, y_s, p_ekv, s_ekv, p_elf, s_elf, p_okv, s_okv, p_win, s_win)
```

```python
import functools
import math

import jax
import jax.numpy as jnp
from jax import lax
from jax.experimental import pallas as pl
from jax.experimental.pallas import tpu as pltpu

F32 = jnp.float32
BF16 = jnp.bfloat16
HIGHEST = lax.Precision.HIGHEST

D_MODEL = 1024
DEPTH = 4
HEAD_DIM = 64
N_HEADS = 16
H_MOBA = 8
H_FOX = 8
MOBA_BLOCK = 256
MOBA_TOPK = 3
NSA_KV_HEADS = 4
NSA_GROUP = 4
CMP_LEN = 32
CMP_STRIDE = 16
SEL_BLOCK = 64
SEL_TOPN = 16
WINDOW = 512
N_BUCKETS = 32
MAX_DISTANCE = 4096
D_FF = 2816
N_EXPERTS = 8
PAGE_SIZE = 128
ALPHA = (2 * DEPTH) ** 0.25
NEG = -1e30
REMOVED = -3e38
FORCED_SCORE = 1e6
LN_EPS = 1e-5
SCALE = HEAD_DIM ** -0.5

LANES = 128
SUBLANES = 8
VMEM_LIMIT = 48 * 1024 * 1024


def _cparams(sem):
    return pltpu.CompilerParams(dimension_semantics=sem, vmem_limit_bytes=VMEM_LIMIT)


def _mm_kernel(x_ref, w_ref, b_ref, o_ref, *, act):
    y = jnp.dot(x_ref[...].astype(BF16), w_ref[...], preferred_element_type=F32)
    if act == "log_sigmoid":
        y = jax.nn.log_sigmoid(y + b_ref[...])
    elif act == "sigmoid":
        y = jax.nn.sigmoid(y + b_ref[...])
    o_ref[...] = y


def _mm(x, w, bias=None, act=None, tn=None):
    M, K = x.shape
    N = w.shape[1]
    tm = min(M, 512)
    tn = tn or min(N, 1024)
    if bias is None:
        bias = jnp.zeros((1, N), F32)
    return pl.pallas_call(
        functools.partial(_mm_kernel, act=act),
        out_shape=jax.ShapeDtypeStruct((M, N), F32),
        grid=(N // tn, M // tm),
        in_specs=[pl.BlockSpec((tm, K), lambda n, m: (m, 0)),
                  pl.BlockSpec((K, tn), lambda n, m: (0, n)),
                  pl.BlockSpec((1, tn), lambda n, m: (0, n))],
        out_specs=pl.BlockSpec((tm, tn), lambda n, m: (m, n)),
        compiler_params=_cparams(("parallel", "parallel")),
        name="proj",
    )(x, w, bias)


def _layer_norm(z, g, b):
    mu = jnp.mean(z, axis=-1, keepdims=True)
    d = z - mu
    var = jnp.mean(d * d, axis=-1, keepdims=True)
    return d * lax.rsqrt(var + LN_EPS) * g + b


def _mm_ln_kernel(*refs, n_br):
    o_refs = refs[:n_br]
    gate_ref, w_ref, x_ref, g_ref, b_ref, y_ref = refs[n_br:]
    if n_br == 1:
        o = o_refs[0][...]
    else:
        gates = gate_ref[...]
        parts = []
        for h in range(N_HEADS):
            acc = None
            for br in range(n_br):
                c = h * n_br + br
                t = gates[:, c:c + 1] * o_refs[br][:, h * HEAD_DIM:(h + 1) * HEAD_DIM]
                acc = t if acc is None else acc + t
            parts.append(acc)
        o = jnp.concatenate(parts, axis=1)
    h = jnp.dot(o.astype(BF16), w_ref[...], preferred_element_type=F32)
    y_ref[...] = _layer_norm(ALPHA * x_ref[...] + h, g_ref[...], b_ref[...])


def _mm_ln(o_list, gates, w, x, g, b):
    M = x.shape[0]
    n_br = len(o_list)
    tm = min(M, 256)
    if gates is None:
        gates = jnp.zeros((M, LANES), F32)
    row = lambda m: (m, 0)
    fixed = lambda m: (0, 0)
    return pl.pallas_call(
        functools.partial(_mm_ln_kernel, n_br=n_br),
        out_shape=jax.ShapeDtypeStruct((M, D_MODEL), F32),
        grid=(M // tm,),
        in_specs=[pl.BlockSpec((tm, D_MODEL), row)] * n_br + [
            pl.BlockSpec((tm, LANES), row),
            pl.BlockSpec((D_MODEL, D_MODEL), fixed),
            pl.BlockSpec((tm, D_MODEL), row),
            pl.BlockSpec((1, D_MODEL), fixed),
            pl.BlockSpec((1, D_MODEL), fixed)],
        out_specs=pl.BlockSpec((tm, D_MODEL), row),
        compiler_params=_cparams(("parallel",)),
        name="out_proj_ln",
    )(*o_list, gates, w, x, g.reshape(1, -1), b.reshape(1, -1))


def _ffn_kernel(x_ref, gate_ref, wa_ref, wb_ref, w2_ref, g_ref, b_ref, y_ref, xb_ref, acc_ref, *, gated):
    e = pl.program_id(1)
    f = pl.program_id(2)

    @pl.when((e == 0) & (f == 0))
    def _():
        xb_ref[...] = x_ref[...].astype(BF16)
        acc_ref[...] = jnp.zeros_like(acc_ref)

    xb = xb_ref[...]
    a = jnp.dot(xb, wa_ref[0], preferred_element_type=F32)
    bb = jnp.dot(xb, wb_ref[0], preferred_element_type=F32)
    hh = jax.nn.silu(a) * bb
    part = jnp.dot(hh.astype(BF16), w2_ref[0], preferred_element_type=F32)
    if gated:
        gates = gate_ref[...]
        lane = lax.broadcasted_iota(jnp.int32, gates.shape, 1)
        part = part * jnp.sum(jnp.where(lane == e, gates, 0.0), axis=1, keepdims=True)
    acc_ref[...] += part

    @pl.when((e == pl.num_programs(1) - 1) & (f == pl.num_programs(2) - 1))
    def _():
        y_ref[...] = _layer_norm(ALPHA * x_ref[...] + acc_ref[...], g_ref[...], b_ref[...])


def _ffn(x, gates, w13, w2, g, b):
    M = x.shape[0]
    E = w13.shape[0]
    tm = min(M, 1024)
    tf = 256
    nf = D_FF // tf
    gated = gates is not None
    if gates is None:
        gates = jnp.zeros((M, LANES), F32)
    return pl.pallas_call(
        functools.partial(_ffn_kernel, gated=gated),
        out_shape=jax.ShapeDtypeStruct((M, D_MODEL), F32),
        grid=(M // tm, E, nf),
        in_specs=[pl.BlockSpec((tm, D_MODEL), lambda m, e, f: (m, 0)),
                  pl.BlockSpec((tm, LANES), lambda m, e, f: (m, 0)),
                  pl.BlockSpec((1, D_MODEL, tf), lambda m, e, f: (e, 0, f)),
                  pl.BlockSpec((1, D_MODEL, tf), lambda m, e, f: (e, 0, nf + f)),
                  pl.BlockSpec((1, tf, D_MODEL), lambda m, e, f: (e, f, 0)),
                  pl.BlockSpec((1, D_MODEL), lambda m, e, f: (0, 0)),
                  pl.BlockSpec((1, D_MODEL), lambda m, e, f: (0, 0))],
        out_specs=pl.BlockSpec((tm, D_MODEL), lambda m, e, f: (m, 0)),
        scratch_shapes=[pltpu.VMEM((tm, D_MODEL), BF16), pltpu.VMEM((tm, D_MODEL), F32)],
        compiler_params=_cparams(("parallel", "arbitrary", "arbitrary")),
        name="ffn_ln",
    )(x, gates, w13, w13, w2, g.reshape(1, -1), b.reshape(1, -1))


def _router_kernel(x_ref, w_ref, o_ref):
    logits = jnp.dot(x_ref[...].astype(BF16), w_ref[...].astype(BF16), preferred_element_type=F32)
    lane = lax.broadcasted_iota(jnp.int32, logits.shape, 1).astype(F32)
    s = jnp.where(lane < N_EXPERTS, logits, REMOVED)
    m1 = jnp.max(s, axis=1, keepdims=True)
    i1 = jnp.min(jnp.where(s == m1, lane, 1e9), axis=1, keepdims=True)
    s2 = jnp.where(lane == i1, REMOVED, s)
    m2 = jnp.max(s2, axis=1, keepdims=True)
    i2 = jnp.min(jnp.where(s2 == m2, lane, 1e9), axis=1, keepdims=True)
    e2 = jnp.exp(m2 - m1)
    den = 1.0 + e2
    o_ref[...] = jnp.where(lane == i1, 1.0 / den, 0.0) + jnp.where(lane == i2, e2 / den, 0.0)


def _router(x, w_router):
    M = x.shape[0]
    tm = min(M, 512)
    w = jnp.pad(w_router, ((0, 0), (0, LANES - N_EXPERTS)))
    return pl.pallas_call(
        _router_kernel,
        out_shape=jax.ShapeDtypeStruct((M, LANES), F32),
        grid=(M // tm,),
        in_specs=[pl.BlockSpec((tm, D_MODEL), lambda m: (m, 0)),
                  pl.BlockSpec((D_MODEL, LANES), lambda m: (0, 0))],
        out_specs=pl.BlockSpec((tm, LANES), lambda m: (m, 0)),
        compiler_params=_cparams(("parallel",)),
        name="router",
    )(x, w)


def _kmean_kernel(k_ref, o_ref, *, nb):
    k = k_ref[0, 0].reshape(nb, MOBA_BLOCK, HEAD_DIM)
    o_ref[0, 0] = jnp.sum(k, axis=1) * (1.0 / MOBA_BLOCK)


def _kmean(k, n_heads):
    B, _, L, _ = k.shape
    nb = L // MOBA_BLOCK
    km = pl.pallas_call(
        functools.partial(_kmean_kernel, nb=nb),
        out_shape=jax.ShapeDtypeStruct((B, n_heads, nb, HEAD_DIM), F32),
        grid=(B, n_heads),
        in_specs=[pl.BlockSpec((1, 1, L, HEAD_DIM), lambda b, h: (b, h, 0, 0))],
        out_specs=pl.BlockSpec((1, 1, nb, HEAD_DIM), lambda b, h: (b, h, 0, 0)),
        compiler_params=_cparams(("parallel", "parallel")),
        name="moba_kmean",
    )(k)
    return jnp.pad(km, ((0, 0), (0, 0), (0, LANES - nb), (0, 0)))


def _cumsum_kernel(x_ref, o_ref, *, chunk):
    r = lax.broadcasted_iota(jnp.int32, (chunk, chunk), 0)
    c = lax.broadcasted_iota(jnp.int32, (chunk, chunk), 1)
    tri = (r <= c).astype(F32)

    def body(j, carry):
        cols = pl.ds(pl.multiple_of(j * chunk, chunk), chunk)
        cs = jnp.dot(x_ref[0, :, cols], tri, preferred_element_type=F32, precision=HIGHEST) + carry
        o_ref[0, :, cols] = cs
        return cs[:, chunk - 1:chunk]

    lax.fori_loop(0, x_ref.shape[2] // chunk, body, jnp.zeros((x_ref.shape[1], 1), F32))


def _cumsum_lanes(x):
    B, H, L = x.shape
    return pl.pallas_call(
        functools.partial(_cumsum_kernel, chunk=256),
        out_shape=jax.ShapeDtypeStruct((B, H, L), F32),
        grid=(B,),
        in_specs=[pl.BlockSpec((1, H, L), lambda b: (b, 0, 0))],
        out_specs=pl.BlockSpec((1, H, L), lambda b: (b, 0, 0)),
        compiler_params=_cparams(("parallel",)),
        name="fox_cumsum",
    )(x)


def _topk_mask(score, k, lane_f):
    sel = jnp.zeros_like(score)
    for _ in range(k):
        mx = jnp.max(score, axis=1, keepdims=True)
        first = jnp.min(jnp.where(score == mx, lane_f, 1e9), axis=1, keepdims=True)
        pick = lane_f == first
        sel = jnp.where(pick, 1.0, sel)
        score = jnp.where(pick, REMOVED, score)
    return sel


def _toeplitz_rows(vec, r0, rc, tk):
    w = vec.shape[1]
    big = jnp.broadcast_to(vec, (rc, w))
    return pltpu.roll(big, tk + 1 + r0, 1, stride=1, stride_axis=0)[:, :tk]


def _flash_kernel(*refs, mode, bias_kind, hb, g, tq, tk, qbase, kbase, dmax, ktile, kactive, klast, two_pass, nk):
    it = iter(refs)
    q_ref, k_ref, v_ref = next(it), next(it), next(it)
    bias_ref = next(it) if bias_kind else None
    aux_ref = next(it) if mode in ("moba", "fox", "sel") else None
    cq_ref = next(it) if mode == "fox" else None
    o_ref, m_ref, l_ref, acc_ref = next(it), next(it), next(it), next(it)
    sel_ref = next(it) if mode == "moba" else None

    qi = pl.program_id(2)
    kk = pl.program_id(3)
    kt = ktile(qi, kk)
    q0 = qbase + qi * tq
    k0 = kbase + kt * tk

    @pl.when(kk == 0)
    def _init():
        m_ref[...] = jnp.full(m_ref.shape, NEG, F32)
        l_ref[...] = jnp.zeros_like(l_ref)
        acc_ref[...] = jnp.zeros_like(acc_ref)
        if mode == "moba":
            nb = aux_ref.shape[2]
            blk = lax.broadcasted_iota(jnp.int32, (tq, nb), 1)
            own = (q0 + lax.broadcasted_iota(jnp.int32, (tq, nb), 0)) >> 8
            ok = blk < own
            blk_f = blk.astype(F32)
            for h in range(hb):
                gsc = lax.dot_general(q_ref[0, h].astype(BF16), aux_ref[0, h].astype(BF16),
                                      (((1,), (1,)), ((), ())), preferred_element_type=F32)
                picked = _topk_mask(jnp.where(ok, gsc, NEG), MOBA_TOPK, blk_f)
                sel_ref[h] = jnp.where(ok, picked, 0.0)

    def step(phase):
        rel = lax.broadcasted_iota(jnp.int32, (tq, tk), 0) - lax.broadcasted_iota(jnp.int32, (tq, tk), 1)
        if mode == "sel":
            flags = aux_ref[0, 0].astype(BF16)
            nbf = flags.shape[1]
            blk = lax.broadcasted_iota(jnp.int32, (nbf, tk), 0)
            kb = (k0 + lax.broadcasted_iota(jnp.int32, (nbf, tk), 1)) >> 6
            picked = jnp.dot(flags, (blk == kb).astype(BF16), preferred_element_type=F32)
        if mode == "moba":
            own_f = jnp.where(((q0 + lax.broadcasted_iota(jnp.int32, (tq, 1), 0)) >> 8) == (k0 >> 8), 1.0, 0.0)
        dist = rel + (q0 - k0)
        base_ok = dist >= 0
        if mode == "win":
            base_ok = base_ok & (dist < WINDOW)
        if mode == "sel":
            base_ok = base_ok & (picked > 0.5)
        heads, scores, valids, values = [], [], [], []
        for hk in range(hb // g):
            kb16 = k_ref[0, hk].astype(BF16)
            vb16 = v_ref[0, hk].astype(BF16)
            for gi in range(g):
                h = hk * g + gi
                qs = (q_ref[0, h] * SCALE).astype(BF16)
                s = lax.dot_general(qs, kb16, (((1,), (1,)), ((), ())), preferred_element_type=F32)
                if bias_kind == "toeplitz":
                    off = pl.multiple_of(dmax - (q0 - k0), LANES)
                    s = s + _toeplitz_rows(bias_ref[h, :, pl.ds(off, tq + tk)], 0, tq, tk)
                elif bias_kind == "table":
                    s = s + bias_ref[h]
                if mode == "fox":
                    s = (s + cq_ref[0, h]) - aux_ref[0, h]
                if mode == "moba":
                    sel = sel_ref[h]
                    lane = lax.broadcasted_iota(jnp.int32, sel.shape, 1)
                    selcol = jnp.sum(jnp.where(lane == (k0 >> 8), sel, 0.0), axis=1, keepdims=True)
                    valid = (own_f * jnp.where(base_ok, 1.0, 0.0) + (1.0 - own_f) * selcol) > 0.5
                else:
                    valid = base_ok
                heads.append(h)
                scores.append(jnp.where(valid, s, NEG))
                valids.append(valid)
                values.append(vb16)
        if phase == "weighted":
            for h, s, valid, vb16 in zip(heads, scores, valids, values):
                p = jnp.where(valid, jnp.exp(s - m_ref[h]), 0.0) / jnp.maximum(l_ref[h], 1e-30)
                acc_ref[h] = acc_ref[h] + jnp.dot(p.astype(BF16), vb16, preferred_element_type=F32)
            return
        m_prev = [m_ref[h] for h in heads]
        m_new = [jnp.maximum(mp, jnp.max(s, axis=1, keepdims=True)) for mp, s in zip(m_prev, scores)]
        probs = [jnp.exp(s - mn) for s, mn in zip(scores, m_new)]
        alphas = [jnp.exp(mp - mn) for mp, mn in zip(m_prev, m_new)]
        sums = [jnp.sum(p, axis=1, keepdims=True) for p in probs]
        if phase == "online":
            pvs = [jnp.dot(p.astype(BF16), vb16, preferred_element_type=F32) for p, vb16 in zip(probs, values)]
        for i, h in enumerate(heads):
            l_ref[h] = alphas[i] * l_ref[h] + sums[i]
            m_ref[h] = m_new[i]
            if phase == "online":
                acc_ref[h] = alphas[i] * acc_ref[h] + pvs[i]

    if two_pass:

        @pl.when(kk < nk)
        def _stats():
            step("stats")

        @pl.when(kk >= nk)
        def _weighted():
            step("weighted")

        @pl.when(kk == 2 * nk - 1)
        def _fin2():
            for h in range(hb):
                o_ref[0, h] = acc_ref[h]
    else:
        @pl.when(kactive(qi, kk))
        def _step():
            step("online")

        @pl.when(kk == klast(qi))
        def _fin():
            for h in range(hb):
                o_ref[0, h] = acc_ref[h] / jnp.maximum(l_ref[h], 1e-30)


def _flash(mode, q, k, v, *, tq, tk, hb, g=1, qbase=0, kbase=0, bias=None, bias_kind=None, aux=None, cq=None,
           two_pass=False, n_heads=None, q_hoff=0, k_hoff=0, v_hoff=0):
    B, _, Tq, _ = q.shape
    H = n_heads or q.shape[1]
    Lk = k.shape[2]
    nq = Tq // tq
    hkb = hb // g
    qo, ko, vo = q_hoff // hb, k_hoff // hkb, v_hoff // hkb
    dmax = 0
    if mode == "win" and Lk > tk:
        nk = WINDOW // tk + tq // tk
        ktile = lambda qi, kk: jnp.maximum(qi * (tq // tk) - WINDOW // tk + kk, 0)
        kactive = lambda qi, kk: qi * (tq // tk) - WINDOW // tk + kk >= 0
        klast = lambda qi: nk - 1
    elif mode == "win":
        nk = 1
        ktile = lambda qi, kk: kk
        kactive = lambda qi, kk: kk >= 0
        klast = lambda qi: 0
    else:
        nk = Lk // tk
        last = lambda qi: jnp.minimum((qbase + qi * tq + tq - 1 - kbase) // tk, nk - 1)
        ktile = lambda qi, kk: jnp.minimum(kk, last(qi))
        kactive = lambda qi, kk: kk <= last(qi)
        klast = last
    if bias_kind == "toeplitz":
        dmax = bias.shape[2] - (tq + tk)
    vtile = ktile
    nsteps = nk
    if two_pass:
        assert nq == 1 and (mode == "win" or (qbase + tq - 1 - kbase) // tk >= nk - 1)
        ktile = lambda qi, kk: kk % nk
        vtile = lambda qi, kk: jnp.maximum(kk - nk, 0)
        nsteps = 2 * nk

    kv_spec = pl.BlockSpec((1, hkb, tk, HEAD_DIM), lambda b, hg, qi, kk: (b, ko + hg, ktile(qi, kk), 0))
    v_spec = pl.BlockSpec((1, hkb, tk, HEAD_DIM), lambda b, hg, qi, kk: (b, vo + hg, vtile(qi, kk), 0))
    in_specs = [pl.BlockSpec((1, hb, tq, HEAD_DIM), lambda b, hg, qi, kk: (b, qo + hg, qi, 0)), kv_spec, v_spec]
    args = [q, k, v]
    if bias_kind == "toeplitz":
        in_specs.append(pl.BlockSpec((hb, 1, bias.shape[2]), lambda b, hg, qi, kk: (hg, 0, 0)))
        args.append(bias)
    elif bias_kind == "table":
        in_specs.append(pl.BlockSpec((hb, tq, tk), lambda b, hg, qi, kk: (hg, qi, ktile(qi, kk))))
        args.append(bias)
    scratch = [pltpu.VMEM((hb, tq, 1), F32), pltpu.VMEM((hb, tq, 1), F32), pltpu.VMEM((hb, tq, HEAD_DIM), F32)]
    if mode == "moba":
        nb = aux.shape[2]
        in_specs.append(pl.BlockSpec((1, hb, nb, HEAD_DIM), lambda b, hg, qi, kk: (b, hg, 0, 0)))
        args.append(aux)
        scratch.append(pltpu.VMEM((hb, tq, nb), F32))
    elif mode == "fox":
        in_specs.append(pl.BlockSpec((1, hb, 1, tk), lambda b, hg, qi, kk: (b, hg, 0, ktile(qi, kk))))
        in_specs.append(pl.BlockSpec((1, hb, tq, 1), lambda b, hg, qi, kk: (b, hg, qi, 0)))
        args += [aux, cq]
    elif mode == "sel":
        in_specs.append(pl.BlockSpec((1, 1, tq, aux.shape[3]), lambda b, hg, qi, kk: (b, hg, qi, 0)))
        args.append(aux)
    return pl.pallas_call(
        functools.partial(_flash_kernel, mode=mode, bias_kind=bias_kind, hb=hb, g=g, tq=tq, tk=tk,
                          qbase=qbase, kbase=kbase, dmax=dmax, ktile=ktile, kactive=kactive, klast=klast,
                          two_pass=two_pass, nk=nk),
        out_shape=jax.ShapeDtypeStruct((B, H, Tq, HEAD_DIM), F32),
        grid=(B, H // hb, nq, nsteps),
        in_specs=in_specs,
        out_specs=pl.BlockSpec((1, hb, tq, HEAD_DIM), lambda b, hg, qi, kk: (b, hg, qi, 0)),
        scratch_shapes=scratch,
        compiler_params=_cparams(("parallel", "parallel", "parallel", "arbitrary")),
        name=f"flash_{mode}_{tq}",
    )(*args)


def _compress_kernel(x_ref, pe_ref, wab_ref, w2_ref, o_ref):
    nch = x_ref.shape[2]
    half = CMP_STRIDE * HEAD_DIM
    x = x_ref[0, 0]
    wab = wab_ref[...].astype(BF16)
    first = jnp.dot((x + pe_ref[:, :half]).astype(BF16), wab[:, :HEAD_DIM], preferred_element_type=F32)
    second = jnp.dot((x + pe_ref[:, half:]).astype(BF16), wab[:, HEAD_DIM:], preferred_element_type=F32)
    hmid = first + pltpu.roll(second, nch - 1, 0)
    o_ref[0, 0] = jnp.dot(jax.nn.silu(hmid).astype(BF16), w2_ref[...].astype(BF16), preferred_element_type=F32)


def _compress(x, slot0, nch, pe, w1, w2):
    B = x.shape[0]
    half = CMP_STRIDE * HEAD_DIM
    wab = jnp.concatenate([w1[:CMP_STRIDE].reshape(half, HEAD_DIM), w1[CMP_STRIDE:].reshape(half, HEAD_DIM)], axis=1)
    return pl.pallas_call(
        _compress_kernel,
        out_shape=jax.ShapeDtypeStruct((B, NSA_KV_HEADS, nch, HEAD_DIM), F32),
        grid=(B, NSA_KV_HEADS),
        in_specs=[pl.BlockSpec((1, 1, nch, half), lambda b, h: (b, slot0 + h, 0, 0)),
                  pl.BlockSpec((1, 2 * half), lambda b, h: (0, 0)),
                  pl.BlockSpec((half, 2 * HEAD_DIM), lambda b, h: (0, 0)),
                  pl.BlockSpec((HEAD_DIM, HEAD_DIM), lambda b, h: (0, 0))],
        out_specs=pl.BlockSpec((1, 1, nch, HEAD_DIM), lambda b, h: (b, h, 0, 0)),
        compiler_params=_cparams(("parallel", "parallel")),
        name="nsa_compress",
    )(x, pe.reshape(1, 2 * half), wab, w2)


def _split3(x):
    hi = x.astype(BF16)
    r1 = x - hi.astype(F32)
    mid = r1.astype(BF16)
    lo = (r1 - mid.astype(F32)).astype(BF16)
    return hi, mid, lo


def _cmp_kernel(q_ref, kc_ref, vc_ref, o_ref, f_ref, *, tq, qbase, nbf):
    qi = pl.program_id(2)
    q0 = qbase + qi * tq
    nc1 = kc_ref.shape[2]
    kc = kc_ref[0, 0].astype(BF16)
    vc = vc_ref[0, 0].astype(BF16)
    qpos = q0 + lax.broadcasted_iota(jnp.int32, (tq, nc1), 0)
    n = lax.broadcasted_iota(jnp.int32, (tq, nc1), 1)
    valid = (n * CMP_STRIDE + (CMP_LEN - 1) <= qpos) & (n < nc1 - 1)
    psum = jnp.zeros((tq, nc1), F32)
    for gi in range(NSA_GROUP):
        qs = (q_ref[0, gi] * SCALE).astype(BF16)
        s = lax.dot_general(qs, kc, (((1,), (1,)), ((), ())), preferred_element_type=F32)
        s = jnp.where(valid, s, NEG)
        m = jnp.max(s, axis=1, keepdims=True)
        p = jnp.where(valid, jnp.exp(s - m), 0.0)
        p = p / jnp.maximum(jnp.sum(p, axis=1, keepdims=True), 1e-30)
        o_ref[0, gi] = jnp.dot(p.astype(BF16), vc, preferred_element_type=F32)
        psum = psum + p
    rn = lax.broadcasted_iota(jnp.int32, (nc1, nbf), 0)
    cj = lax.broadcasted_iota(jnp.int32, (nc1, nbf), 1)
    r = SEL_BLOCK // CMP_STRIDE
    ov = ((rn >= r * cj - 1) & (rn <= r * cj + r - 1) & (rn < nc1 - 1)).astype(BF16)
    pslc = jnp.zeros((tq, nbf), F32)
    for part in _split3(psum):
        pslc = pslc + jnp.dot(part, ov, preferred_element_type=F32)
    blk = lax.broadcasted_iota(jnp.int32, (tq, nbf), 1)
    own = (q0 + lax.broadcasted_iota(jnp.int32, (tq, nbf), 0)) >> 6
    ok = blk <= own
    forced = (blk == 0) | (blk == own) | (blk == own - 1)
    score = jnp.where(ok, jnp.where(forced, FORCED_SCORE, pslc), NEG)
    picked = _topk_mask(score, SEL_TOPN, blk.astype(F32))
    f_ref[0, 0] = jnp.where(ok, picked, 0.0)


def _cmp_attention(q, kc, vc, *, tq, qbase, nbf):
    B, H, Tq, _ = q.shape
    nc1 = kc.shape[2]
    kv_spec = pl.BlockSpec((1, 1, nc1, HEAD_DIM), lambda b, h, qi: (b, h, 0, 0))
    return pl.pallas_call(
        functools.partial(_cmp_kernel, tq=tq, qbase=qbase, nbf=nbf),
        out_shape=(jax.ShapeDtypeStruct((B, H, Tq, HEAD_DIM), F32),
                   jax.ShapeDtypeStruct((B, NSA_KV_HEADS, Tq, nbf), F32)),
        grid=(B, NSA_KV_HEADS, Tq // tq),
        in_specs=[pl.BlockSpec((1, NSA_GROUP, tq, HEAD_DIM), lambda b, h, qi: (b, h, qi, 0)), kv_spec, kv_spec],
        out_specs=(pl.BlockSpec((1, NSA_GROUP, tq, HEAD_DIM), lambda b, h, qi: (b, h, qi, 0)),
                   pl.BlockSpec((1, 1, tq, nbf), lambda b, h, qi: (b, h, qi, 0))),
        compiler_params=_cparams(("parallel", "parallel", "parallel")),
        name=f"nsa_cmp_{tq}",
    )(q, kc, vc)


def _gather_kernel(pt_ref, cache_ref, new_ref, o_ref, *chunk_refs, n_pages, n_slots, chunk_slots):
    p = pl.program_id(1)
    nh = n_slots // 2

    @pl.when(p < n_pages)
    def _():
        for s in range(n_slots):
            o_ref[0, s] = cache_ref[0, 0, :, s // nh, s % nh, :]
        for i, s in enumerate(chunk_slots):
            for l in range(CMP_STRIDE):
                rows = pl.ds(l, PAGE_SIZE // CMP_STRIDE, stride=CMP_STRIDE)
                chunk_refs[0][0, i, :, l * HEAD_DIM:(l + 1) * HEAD_DIM] = cache_ref[0, 0, rows, s // nh, s % nh, :]

    @pl.when(p >= n_pages)
    def _():
        for s in range(n_slots):
            o_ref[0, s] = new_ref[0, :, s * HEAD_DIM:(s + 1) * HEAD_DIM]
        if chunk_slots:
            chunk_refs[0][...] = jnp.zeros_like(chunk_refs[0])


def _gather_pages(cache, page_table, layer, new_rows, chunk_slots=()):
    B, n_pages = page_table.shape
    nh = cache.shape[4]
    n_slots = 2 * nh
    C = n_slots * HEAD_DIM
    extra = new_rows.shape[1] // PAGE_SIZE
    L = (n_pages + extra) * PAGE_SIZE
    lines = PAGE_SIZE // CMP_STRIDE
    out_shape = [jax.ShapeDtypeStruct((B, n_slots, L, HEAD_DIM), F32)]
    out_specs = [pl.BlockSpec((1, n_slots, PAGE_SIZE, HEAD_DIM), lambda b, p, pt: (b, 0, p, 0))]
    if chunk_slots:
        out_shape.append(jax.ShapeDtypeStruct((B, len(chunk_slots), L // CMP_STRIDE, CMP_STRIDE * HEAD_DIM), F32))
        out_specs.append(pl.BlockSpec((1, len(chunk_slots), lines, CMP_STRIDE * HEAD_DIM),
                                      lambda b, p, pt: (b, 0, p, 0)))
    grid_spec = pltpu.PrefetchScalarGridSpec(
        num_scalar_prefetch=1,
        grid=(B, n_pages + extra),
        in_specs=[pl.BlockSpec((1, 1, PAGE_SIZE, 2, nh, HEAD_DIM),
                               lambda b, p, pt: (pt[b, jnp.minimum(p, n_pages - 1)], layer, 0, 0, 0, 0)),
                  pl.BlockSpec((1, PAGE_SIZE, C), lambda b, p, pt: (b, jnp.maximum(p - n_pages, 0), 0))],
        out_specs=out_specs,
    )
    out = pl.pallas_call(
        functools.partial(_gather_kernel, n_pages=n_pages, n_slots=n_slots, chunk_slots=tuple(chunk_slots)),
        out_shape=out_shape,
        grid_spec=grid_spec,
        compiler_params=_cparams(("parallel", "arbitrary")),
        name="gather_pages",
    )(page_table, cache, new_rows)
    return out if chunk_slots else out[0]


PAGES_PER_STEP = 8


def _gather_flat_kernel(pt_ref, *refs):
    o_ref = refs[-1]
    for i, c_ref in enumerate(refs[:-1]):
        o_ref[0, i] = c_ref[0, 0]


def _gather_logf(cache, page_table, layer):
    B, n_pages = page_table.shape
    n_pool, nl, ps, w = cache.shape
    flat = cache.reshape(n_pool, nl, 1, ps * w)
    spec = lambda i: pl.BlockSpec((1, 1, 1, ps * w),
                                  lambda b, s, pt: (pt[b, s * PAGES_PER_STEP + i], layer, 0, 0))
    grid_spec = pltpu.PrefetchScalarGridSpec(
        num_scalar_prefetch=1,
        grid=(B, n_pages // PAGES_PER_STEP),
        in_specs=[spec(i) for i in range(PAGES_PER_STEP)],
        out_specs=pl.BlockSpec((1, PAGES_PER_STEP, 1, ps * w), lambda b, s, pt: (b, s, 0, 0)),
    )
    out = pl.pallas_call(
        _gather_flat_kernel,
        out_shape=jax.ShapeDtypeStruct((B, n_pages, 1, ps * w), F32),
        grid_spec=grid_spec,
        compiler_params=_cparams(("parallel", "arbitrary")),
        name="gather_logf",
    )(page_table, *([flat] * PAGES_PER_STEP))
    return out.reshape(B, n_pages * ps, w)


def _rel_bucket(dist):
    n = jnp.maximum(dist, 0)
    max_exact = N_BUCKETS // 2
    nf = jnp.maximum(n, 1).astype(F32)
    large = max_exact + (jnp.log(nf / max_exact) / math.log(MAX_DISTANCE / max_exact)
                         * (N_BUCKETS - max_exact)).astype(jnp.int32)
    large = jnp.minimum(large, N_BUCKETS - 1)
    return jnp.where(n < max_exact, n, large)


def _reversed_bias_table(rel_tab, dmax, tq, tk):
    length = dmax + tq + tk
    d = dmax + tq - 1 - jnp.arange(length)
    vals = jnp.where(d >= 0, rel_tab[:, _rel_bucket(d)], 0.0)
    return vals[:, None, :].astype(F32)


def _decode_bias_table(rel_tab, qbase, kbase, tq, lk):
    d = (qbase + jnp.arange(tq))[:, None] - (kbase + jnp.arange(lk))[None, :]
    return jnp.where(d >= 0, rel_tab[:, _rel_bucket(d)], 0.0).astype(F32)


def _to_heads(x, n):
    B, T, _ = x.shape
    return x.reshape(B, T, n, HEAD_DIM).transpose(0, 2, 1, 3)


def _from_heads(x):
    B, H, T, _ = x.shape
    return x.transpose(0, 2, 1, 3).reshape(B * T, H * HEAD_DIM)


def _pad_rows(x, rows):
    return jnp.pad(x, ((0, 0), (0, rows - x.shape[1])) + ((0, 0),) * (x.ndim - 2))


DEC_TQ = 8


def _decode_key_tile(lk):
    return lk // 3 if lk % (3 * LANES) == 0 else lk // 2


def _even_layer(x, B, T, sample, j, p):
    nq = N_HEADS * HEAD_DIM
    w_in = p["w_in_even"][j]
    q = _mm(x, w_in[:, :nq].astype(BF16))
    kv = _mm(x, w_in[:, nq:3 * nq].astype(BF16))
    wf = jnp.pad(w_in[:, 3 * nq:], ((0, 0), (0, LANES - H_FOX))).astype(BF16)
    bf = jnp.pad(p["b_forget"][j], (0, LANES - H_FOX)).reshape(1, LANES)
    logf = _mm(x, wf, bias=bf, act="log_sigmoid")[:, :H_FOX]
    kv3 = kv.reshape(B, T, 2 * nq)
    logf3 = logf.reshape(B, T, H_FOX)
    rel_tab = p["rel_tab"]
    if not sample:
        qh = _to_heads(q.reshape(B, T, nq), N_HEADS)
        kvh = _to_heads(kv3, 2 * N_HEADS)
        kmean = _kmean(kvh, H_MOBA)
        t = MOBA_BLOCK
        rv = _reversed_bias_table(rel_tab[:H_MOBA], T - t, t, t)
        o_a = _flash("moba", qh, kvh, kvh, tq=t, tk=t, hb=4, n_heads=H_MOBA, v_hoff=N_HEADS,
                     bias=rv, bias_kind="toeplitz", aux=kmean)
        c = _cumsum_lanes(logf3.transpose(0, 2, 1)).reshape(B, H_FOX, 1, T)
        tf = min(512, T)
        o_b = _flash("fox", qh, kvh, kvh, tq=tf, tk=tf, hb=2, n_heads=H_FOX, q_hoff=H_MOBA, k_hoff=H_MOBA,
                     v_hoff=N_HEADS + H_MOBA, aux=c, cq=c.reshape(B, H_FOX, T, 1))
    else:
        past = p["page_table"].shape[1] * PAGE_SIZE
        lk = past + MOBA_BLOCK
        new_rows = _pad_rows(kv3, MOBA_BLOCK)
        kvh = _gather_pages(p["cache_even_kv"], p["page_table"], j, new_rows)
        qh = _to_heads(_pad_rows(q.reshape(B, T, nq), DEC_TQ), N_HEADS)
        kmean = _kmean(kvh, H_MOBA)
        tb = _decode_bias_table(rel_tab[:H_MOBA], past, 0, DEC_TQ, lk)
        o_a = _flash("moba", qh, kvh, kvh, tq=DEC_TQ, tk=MOBA_BLOCK, hb=8, n_heads=H_MOBA, v_hoff=N_HEADS,
                     qbase=past, bias=tb, bias_kind="table", aux=kmean, two_pass=True)
        lf_past = _gather_logf(p["cache_even_logf"], p["page_table"], j)
        lf_all = jnp.concatenate([lf_past, _pad_rows(logf3, MOBA_BLOCK)], axis=1)
        c = _cumsum_lanes(lf_all.transpose(0, 2, 1)).reshape(B, H_FOX, 1, lk)
        tkf = _decode_key_tile(lk)
        o_b = _flash("fox", qh, kvh, kvh, tq=DEC_TQ, tk=tkf, hb=2, n_heads=H_FOX, q_hoff=H_MOBA, k_hoff=H_MOBA,
                     v_hoff=N_HEADS + H_MOBA, qbase=past, aux=c,
                     cq=c[:, :, 0, past:past + DEC_TQ].reshape(B, H_FOX, DEC_TQ, 1), two_pass=True)
        o_a, o_b = o_a[:, :, :T], o_b[:, :, :T]
    o = _from_heads(jnp.concatenate([o_a, o_b], axis=1))
    layer = 2 * j
    x1 = _mm_ln([o], None, p["w_out_even"][j].astype(BF16), x, p["ln_g"][layer, 0], p["ln_b"][layer, 0])
    x2 = _ffn(x1, None, p["ffn_w13"][j:j + 1].astype(BF16), p["ffn_w2"][j:j + 1].astype(BF16),
              p["ln_g"][layer, 1], p["ln_b"][layer, 1])
    return x2, kv3, logf3


def _odd_layer(x, B, T, sample, j, p):
    hk, d = NSA_KV_HEADS, HEAD_DIM
    nq = N_HEADS * d
    nkv = 4 * hk * d
    nw = 2 * hk * d
    w_in = p["w_in_odd"][j]
    q = _mm(x, w_in[:, :nq].astype(BF16))
    kv = _mm(x, w_in[:, nq:nq + nkv].astype(BF16))
    kvw = _mm(x, w_in[:, nq + nkv:nq + nkv + nw].astype(BF16), tn=nw)
    ngate = 3 * N_HEADS
    wg = jnp.pad(w_in[:, nq + nkv + nw:], ((0, 0), (0, LANES - ngate))).astype(BF16)
    bg = jnp.pad(p["b_gate"][j], (0, LANES - ngate)).reshape(1, LANES)
    gates = _mm(x, wg, bias=bg, act="sigmoid")
    kv3 = kv.reshape(B, T, nkv)
    kvw3 = kvw.reshape(B, T, nw)
    rel_tab = p["rel_tab"]
    pe, w1, w2 = p["cmp_pe"][j], p["cmp_w1"][j], p["cmp_w2"][j]
    if not sample:
        qh = _to_heads(q.reshape(B, T, nq), N_HEADS)
        kvh = _to_heads(kv3, 4 * hk)
        chunks = kvh.reshape(B, 4 * hk, T // CMP_STRIDE, CMP_STRIDE * d)
        kc = _compress(chunks, 0, T // CMP_STRIDE, pe[0], w1[0], w2[0])
        vc = _compress(chunks, 2 * hk, T // CMP_STRIDE, pe[1], w1[1], w2[1])
        t = 256
        o_cmp, flags = _cmp_attention(qh, kc, vc, tq=t, qbase=0, nbf=T // SEL_BLOCK)
        rv = _reversed_bias_table(rel_tab, T - t, t, t)
        o_sel = _flash("sel", qh, kvh, kvh, tq=t, tk=t, hb=NSA_GROUP, g=NSA_GROUP, k_hoff=hk, v_hoff=3 * hk,
                       bias=rv, bias_kind="toeplitz", aux=flags)
        kwh = _to_heads(kvw3, 2 * hk)
        o_win = _flash("win", qh, kwh, kwh, tq=t, tk=t, hb=NSA_GROUP, g=NSA_GROUP, v_hoff=hk,
                       bias=rv, bias_kind="toeplitz")
    else:
        past = p["page_table"].shape[1] * PAGE_SIZE
        lk = past + 256
        new_rows = _pad_rows(kv3, 256)
        cmp_slots = tuple(range(hk)) + tuple(range(2 * hk, 3 * hk))
        kvh, chunks = _gather_pages(p["cache_odd_kv"], p["page_table"], j, new_rows, cmp_slots)
        qh = _to_heads(_pad_rows(q.reshape(B, T, nq), DEC_TQ), N_HEADS)
        nch = past // CMP_STRIDE
        assert (past + T) // CMP_STRIDE == nch, "new rows must not complete a compression chunk"
        kc = _compress(chunks, 0, nch, pe[0], w1[0], w2[0])
        vc = _compress(chunks, hk, nch, pe[1], w1[1], w2[1])
        nbf = 256
        o_cmp, flags = _cmp_attention(qh, kc, vc, tq=DEC_TQ, qbase=past, nbf=nbf)
        tb = _decode_bias_table(rel_tab, past, 0, DEC_TQ, lk)
        o_sel = _flash("sel", qh, kvh, kvh, tq=DEC_TQ, tk=_decode_key_tile(lk), hb=NSA_GROUP, g=NSA_GROUP, k_hoff=hk,
                       v_hoff=3 * hk, qbase=past, bias=tb, bias_kind="table", aux=flags, two_pass=True)
        win = p["state_odd_win"][:, j]
        wlen = win.shape[1]
        lw = wlen + LANES
        ext = jnp.concatenate([win.reshape(B, wlen, nw), kvw3], axis=1)
        kwh = _to_heads(_pad_rows(ext, lw), 2 * hk)
        tbw = _decode_bias_table(rel_tab, past, past - wlen, DEC_TQ, lw)
        o_win = _flash("win", qh, kwh, kwh, tq=DEC_TQ, tk=lw, hb=NSA_GROUP, g=NSA_GROUP, v_hoff=hk,
                       qbase=past, kbase=past - wlen, bias=tbw, bias_kind="table", two_pass=True)
        o_cmp, o_sel, o_win = o_cmp[:, :, :T], o_sel[:, :, :T], o_win[:, :, :T]
    layer = 2 * j + 1
    x1 = _mm_ln([_from_heads(o_cmp), _from_heads(o_sel), _from_heads(o_win)], gates,
                p["w_out_odd"][j].astype(BF16), x, p["ln_g"][layer, 0], p["ln_b"][layer, 0])
    rg = _router(x1, p["w_router"][j])
    x2 = _ffn(x1, rg, p["moe_w13"][j].astype(BF16), p["moe_w2"][j].astype(BF16),
              p["ln_g"][layer, 1], p["ln_b"][layer, 1])
    return x2, kv3, kvw3


def _run(x3, sample, p):
    B, T, _ = x3.shape
    x = x3.reshape(B * T, D_MODEL)
    ekv, elf, okv, owin = [], [], [], []
    for layer in range(DEPTH):
        j = layer // 2
        if layer % 2 == 0:
            x, kv_new, lf_new = _even_layer(x, B, T, sample, j, p)
            ekv.append(kv_new.reshape(B, T, 2, N_HEADS, HEAD_DIM))
            elf.append(lf_new)
        else:
            x, kv_new, kvw_new = _odd_layer(x, B, T, sample, j, p)
            okv.append(kv_new.reshape(B, T, 2, 2 * NSA_KV_HEADS, HEAD_DIM))
            kvw5 = kvw_new.reshape(B, T, 2, NSA_KV_HEADS, HEAD_DIM)
            if sample:
                win = p["state_odd_win"][:, j]
                owin.append(jnp.concatenate([win, kvw5], axis=1)[:, -win.shape[1]:])
            else:
                owin.append(kvw5[:, -min(WINDOW, T):])
    return (x.reshape(B, T, D_MODEL), jnp.stack(ekv, 1), jnp.stack(elf, 1), jnp.stack(okv, 1), jnp.stack(owin, 1))


def kernel(x_prompt, x_sample, cache_even_kv, cache_even_logf, cache_odd_kv, state_odd_win, page_table, rel_bias, ln_g, ln_b, w_in_even, b_forget, w_out_even, ffn_w13, ffn_w2, w_in_odd, b_gate, cmp_pe, cmp_w1, cmp_w2, w_out_odd, w_router, moe_w13, moe_w2):
    n_pool = cache_even_kv.shape[0]
    p = dict(
        rel_tab=rel_bias.T,
        page_table=page_table,
        cache_even_kv=cache_even_kv,
        cache_even_logf=cache_even_logf,
        cache_odd_kv=cache_odd_kv,
        state_odd_win=state_odd_win,
        ln_g=ln_g, ln_b=ln_b, w_in_even=w_in_even, b_forget=b_forget, w_out_even=w_out_even,
        ffn_w13=ffn_w13, ffn_w2=ffn_w2, w_in_odd=w_in_odd, b_gate=b_gate, cmp_pe=cmp_pe, cmp_w1=cmp_w1,
        cmp_w2=cmp_w2, w_out_odd=w_out_odd, w_router=w_router, moe_w13=moe_w13, moe_w2=moe_w2)
    y_p, p_ekv, p_elf, p_okv, p_win = _run(x_prompt, False, p)
    y_s, s_ekv, s_elf, s_okv, s_win = _run(x_sample, True, p)
    return (y_p, y_s, p_ekv, s_ekv, p_elf, s_elf, p_okv, s_okv, p_win, s_win)
```

```python
import functools
import math

import jax
import jax.numpy as jnp
from jax import lax
from jax.experimental import pallas as pl
from jax.experimental.pallas import tpu as pltpu

F32 = jnp.float32
BF16 = jnp.bfloat16

D_MODEL = 1024
DEPTH = 4
HEAD_DIM = 64
N_HEADS = 16
H_MOBA = 8
H_FOX = 8
MOBA_BLOCK = 256
MOBA_TOPK = 3
NSA_KV_HEADS = 4
NSA_GROUP = 4
CMP_LEN = 32
CMP_STRIDE = 16
SEL_BLOCK = 64
SEL_TOPN = 16
WINDOW = 512
N_BUCKETS = 32
MAX_DISTANCE = 4096
D_FF = 2816
N_EXPERTS = 8
PAGE_SIZE = 128
ALPHA = (2 * DEPTH) ** 0.25
NEG = -1e30
REMOVED = -3e38
FORCED_SCORE = 1e6
LN_EPS = 1e-5
SCALE = HEAD_DIM ** -0.5

LANES = 128
SUBLANES = 8
VMEM_LIMIT = 48 * 1024 * 1024


def _cparams(sem):
    return pltpu.CompilerParams(dimension_semantics=sem, vmem_limit_bytes=VMEM_LIMIT)


def _mm_kernel(x_ref, w_ref, b_ref, o_ref, *, act):
    y = jnp.dot(x_ref[...].astype(BF16), w_ref[...], preferred_element_type=F32)
    if act == "log_sigmoid":
        y = jax.nn.log_sigmoid(y + b_ref[...])
    elif act == "sigmoid":
        y = jax.nn.sigmoid(y + b_ref[...])
    o_ref[...] = y


def _mm(x, w, bias=None, act=None, tn=None):
    M, K = x.shape
    N = w.shape[1]
    tm = min(M, 512)
    tn = tn or min(N, 1024)
    if bias is None:
        bias = jnp.zeros((1, N), F32)
    return pl.pallas_call(
        functools.partial(_mm_kernel, act=act),
        out_shape=jax.ShapeDtypeStruct((M, N), F32),
        grid=(N // tn, M // tm),
        in_specs=[pl.BlockSpec((tm, K), lambda n, m: (m, 0)),
                  pl.BlockSpec((K, tn), lambda n, m: (0, n)),
                  pl.BlockSpec((1, tn), lambda n, m: (0, n))],
        out_specs=pl.BlockSpec((tm, tn), lambda n, m: (m, n)),
        compiler_params=_cparams(("parallel", "parallel")),
        name="proj",
    )(x, w, bias)


def _layer_norm(z, g, b):
    mu = jnp.mean(z, axis=-1, keepdims=True)
    d = z - mu
    var = jnp.mean(d * d, axis=-1, keepdims=True)
    return d * lax.rsqrt(var + LN_EPS) * g + b


def _mm_ln_kernel(*refs, n_br):
    o_refs = refs[:n_br]
    gate_ref, w_ref, x_ref, g_ref, b_ref, y_ref = refs[n_br:]
    if n_br == 1:
        o = o_refs[0][...]
    else:
        gates = gate_ref[...]
        parts = []
        for h in range(N_HEADS):
            acc = None
            for br in range(n_br):
                c = h * n_br + br
                t = gates[:, c:c + 1] * o_refs[br][:, h * HEAD_DIM:(h + 1) * HEAD_DIM]
                acc = t if acc is None else acc + t
            parts.append(acc)
        o = jnp.concatenate(parts, axis=1)
    h = jnp.dot(o.astype(BF16), w_ref[...], preferred_element_type=F32)
    y_ref[...] = _layer_norm(ALPHA * x_ref[...] + h, g_ref[...], b_ref[...])


def _mm_ln(o_list, gates, w, x, g, b):
    M = x.shape[0]
    n_br = len(o_list)
    tm = min(M, 256)
    if gates is None:
        gates = jnp.zeros((M, LANES), F32)
    row = lambda m: (m, 0)
    fixed = lambda m: (0, 0)
    return pl.pallas_call(
        functools.partial(_mm_ln_kernel, n_br=n_br),
        out_shape=jax.ShapeDtypeStruct((M, D_MODEL), F32),
        grid=(M // tm,),
        in_specs=[pl.BlockSpec((tm, D_MODEL), row)] * n_br + [
            pl.BlockSpec((tm, LANES), row),
            pl.BlockSpec((D_MODEL, D_MODEL), fixed),
            pl.BlockSpec((tm, D_MODEL), row),
            pl.BlockSpec((1, D_MODEL), fixed),
            pl.BlockSpec((1, D_MODEL), fixed)],
        out_specs=pl.BlockSpec((tm, D_MODEL), row),
        compiler_params=_cparams(("parallel",)),
        name="out_proj_ln",
    )(*o_list, gates, w, x, g.reshape(1, -1), b.reshape(1, -1))


def _ffn_kernel(x_ref, gate_ref, wa_ref, wb_ref, w2_ref, g_ref, b_ref, y_ref, xb_ref, acc_ref, *, gated):
    e = pl.program_id(1)
    f = pl.program_id(2)

    @pl.when((e == 0) & (f == 0))
    def _():
        xb_ref[...] = x_ref[...].astype(BF16)
        acc_ref[...] = jnp.zeros_like(acc_ref)

    xb = xb_ref[...]
    a = jnp.dot(xb, wa_ref[0], preferred_element_type=F32)
    bb = jnp.dot(xb, wb_ref[0], preferred_element_type=F32)
    hh = jax.nn.silu(a) * bb
    part = jnp.dot(hh.astype(BF16), w2_ref[0], preferred_element_type=F32)
    if gated:
        gates = gate_ref[...]
        lane = lax.broadcasted_iota(jnp.int32, gates.shape, 1)
        part = part * jnp.sum(jnp.where(lane == e, gates, 0.0), axis=1, keepdims=True)
    acc_ref[...] += part

    @pl.when((e == pl.num_programs(1) - 1) & (f == pl.num_programs(2) - 1))
    def _():
        y_ref[...] = _layer_norm(ALPHA * x_ref[...] + acc_ref[...], g_ref[...], b_ref[...])


def _ffn(x, gates, w13, w2, g, b):
    M = x.shape[0]
    E = w13.shape[0]
    tm = min(M, 1024)
    tf = 256
    nf = D_FF // tf
    gated = gates is not None
    if gates is None:
        gates = jnp.zeros((M, LANES), F32)
    return pl.pallas_call(
        functools.partial(_ffn_kernel, gated=gated),
        out_shape=jax.ShapeDtypeStruct((M, D_MODEL), F32),
        grid=(M // tm, E, nf),
        in_specs=[pl.BlockSpec((tm, D_MODEL), lambda m, e, f: (m, 0)),
                  pl.BlockSpec((tm, LANES), lambda m, e, f: (m, 0)),
                  pl.BlockSpec((1, D_MODEL, tf), lambda m, e, f: (e, 0, f)),
                  pl.BlockSpec((1, D_MODEL, tf), lambda m, e, f: (e, 0, nf + f)),
                  pl.BlockSpec((1, tf, D_MODEL), lambda m, e, f: (e, f, 0)),
                  pl.BlockSpec((1, D_MODEL), lambda m, e, f: (0, 0)),
                  pl.BlockSpec((1, D_MODEL), lambda m, e, f: (0, 0))],
        out_specs=pl.BlockSpec((tm, D_MODEL), lambda m, e, f: (m, 0)),
        scratch_shapes=[pltpu.VMEM((tm, D_MODEL), BF16), pltpu.VMEM((tm, D_MODEL), F32)],
        compiler_params=_cparams(("parallel", "arbitrary", "arbitrary")),
        name="ffn_ln",
    )(x, gates, w13, w13, w2, g.reshape(1, -1), b.reshape(1, -1))


def _router_kernel(x_ref, w_ref, o_ref):
    logits = jnp.dot(x_ref[...].astype(BF16), w_ref[...].astype(BF16), preferred_element_type=F32)
    lane = lax.broadcasted_iota(jnp.int32, logits.shape, 1).astype(F32)
    s = jnp.where(lane < N_EXPERTS, logits, REMOVED)
    m1 = jnp.max(s, axis=1, keepdims=True)
    i1 = jnp.min(jnp.where(s == m1, lane, 1e9), axis=1, keepdims=True)
    s2 = jnp.where(lane == i1, REMOVED, s)
    m2 = jnp.max(s2, axis=1, keepdims=True)
    i2 = jnp.min(jnp.where(s2 == m2, lane, 1e9), axis=1, keepdims=True)
    e2 = jnp.exp(m2 - m1)
    den = 1.0 + e2
    o_ref[...] = jnp.where(lane == i1, 1.0 / den, 0.0) + jnp.where(lane == i2, e2 / den, 0.0)


def _router(x, w_router):
    M = x.shape[0]
    tm = min(M, 512)
    w = jnp.pad(w_router, ((0, 0), (0, LANES - N_EXPERTS)))
    return pl.pallas_call(
        _router_kernel,
        out_shape=jax.ShapeDtypeStruct((M, LANES), F32),
        grid=(M // tm,),
        in_specs=[pl.BlockSpec((tm, D_MODEL), lambda m: (m, 0)),
                  pl.BlockSpec((D_MODEL, LANES), lambda m: (0, 0))],
        out_specs=pl.BlockSpec((tm, LANES), lambda m: (m, 0)),
        compiler_params=_cparams(("parallel",)),
        name="router",
    )(x, w)


def _kmean_kernel(k_ref, o_ref, *, nb):
    k = k_ref[0, 0].reshape(nb, MOBA_BLOCK, HEAD_DIM)
    o_ref[0, 0] = jnp.sum(k, axis=1) * (1.0 / MOBA_BLOCK)


def _kmean(k, n_heads):
    B, _, L, _ = k.shape
    nb = L // MOBA_BLOCK
    km = pl.pallas_call(
        functools.partial(_kmean_kernel, nb=nb),
        out_shape=jax.ShapeDtypeStruct((B, n_heads, nb, HEAD_DIM), F32),
        grid=(B, n_heads),
        in_specs=[pl.BlockSpec((1, 1, L, HEAD_DIM), lambda b, h: (b, h, 0, 0))],
        out_specs=pl.BlockSpec((1, 1, nb, HEAD_DIM), lambda b, h: (b, h, 0, 0)),
        compiler_params=_cparams(("parallel", "parallel")),
        name="moba_kmean",
    )(k)
    return jnp.pad(km, ((0, 0), (0, 0), (0, LANES - nb), (0, 0)))


def _seq_scan_kernel(x_ref, o_ref):
    def body(i, acc):
        acc = acc + x_ref[i]
        o_ref[i] = acc
        return acc

    lax.fori_loop(0, x_ref.shape[0], body, jnp.zeros(x_ref.shape[1:], F32))


def _seq_scan(x):
    N, R, C = x.shape
    spec = pl.BlockSpec((N, R, LANES), lambda c: (0, 0, c))
    return pl.pallas_call(
        _seq_scan_kernel, out_shape=jax.ShapeDtypeStruct(x.shape, F32), grid=(C // LANES,),
        in_specs=[spec], out_specs=spec, compiler_params=_cparams(("parallel",)), name="fox_scan",
    )(x)


def _add_rows_kernel(x_ref, y_ref, o_ref):
    o_ref[...] = x_ref[...] + y_ref[...]


def _pad_lanes(x):
    return jnp.pad(x, ((0, 0),) * (x.ndim - 1) + ((0, -x.shape[-1] % LANES),))


SCAN_BASE = 128


def _cumsum_lanes(x):
    B, H, L = x.shape
    nb = L // SCAN_BASE
    assert nb * SCAN_BASE == L and nb <= SCAN_BASE
    cols = nb * H
    xs = _pad_lanes(x.reshape(B, H, nb, SCAN_BASE).transpose(3, 0, 2, 1).reshape(SCAN_BASE, B, cols))
    inner = _seq_scan(xs)
    tot = inner[SCAN_BASE - 1, :, :cols].reshape(B, nb, H).transpose(1, 0, 2).reshape(nb, 1, B * H)
    incl = _seq_scan(_pad_lanes(tot))[:, 0, :B * H]
    offs = jnp.concatenate([jnp.zeros((1, B * H), F32), incl[:-1]], axis=0)
    offs = _pad_lanes(offs.reshape(nb, B, H).transpose(1, 0, 2).reshape(1, B, cols))
    cp = xs.shape[2]
    out = pl.pallas_call(
        _add_rows_kernel, out_shape=jax.ShapeDtypeStruct(xs.shape, F32), grid=(cp // LANES,),
        in_specs=[pl.BlockSpec((SCAN_BASE, B, LANES), lambda c: (0, 0, c)),
                  pl.BlockSpec((1, B, LANES), lambda c: (0, 0, c))],
        out_specs=pl.BlockSpec((SCAN_BASE, B, LANES), lambda c: (0, 0, c)),
        compiler_params=_cparams(("parallel",)), name="fox_scan_add",
    )(inner, offs)
    return out[:, :, :cols].reshape(SCAN_BASE, B, nb, H).transpose(1, 3, 2, 0).reshape(B, H, L)


def _topk_mask(score, k, lane_f):
    sel = jnp.zeros_like(score)
    for _ in range(k):
        mx = jnp.max(score, axis=1, keepdims=True)
        first = jnp.min(jnp.where(score == mx, lane_f, 1e9), axis=1, keepdims=True)
        pick = lane_f == first
        sel = jnp.where(pick, 1.0, sel)
        score = jnp.where(pick, REMOVED, score)
    return sel


def _toeplitz_rows(vec, r0, rc, tk):
    w = vec.shape[1]
    big = jnp.broadcast_to(vec, (rc, w))
    return pltpu.roll(big, tk + 1 + r0, 1, stride=1, stride_axis=0)[:, :tk]


def _flash_kernel(*refs, mode, bias_kind, hb, g, tq, tk, qbase, kbase, dmax, ktile, kactive, klast, two_pass, nk,
                  pair_tables):
    it = iter(refs)
    if pair_tables:
        qtab_ref, ktab_ref = next(it), next(it)
    q_ref, k_ref, v_ref = next(it), next(it), next(it)
    bias_ref = next(it) if bias_kind else None
    aux_ref = next(it) if mode in ("moba", "fox", "sel") else None
    cq_ref = next(it) if mode == "fox" else None
    o_ref, m_ref, l_ref, acc_ref = next(it), next(it), next(it), next(it)
    sel_ref = next(it) if mode == "moba" else None

    if pair_tables:
        qi = qtab_ref[pl.program_id(2)]
        kk = ktab_ref[pl.program_id(2)]
    else:
        qi = pl.program_id(2)
        kk = pl.program_id(3)
    kt = ktile(qi, kk)
    q0 = qbase + qi * tq
    k0 = kbase + kt * tk

    @pl.when(kk == 0)
    def _init():
        m_ref[...] = jnp.full(m_ref.shape, NEG, F32)
        l_ref[...] = jnp.zeros_like(l_ref)
        acc_ref[...] = jnp.zeros_like(acc_ref)
        if mode == "moba":
            nb = aux_ref.shape[2]
            blk = lax.broadcasted_iota(jnp.int32, (tq, nb), 1)
            own = (q0 + lax.broadcasted_iota(jnp.int32, (tq, nb), 0)) >> 8
            ok = blk < own
            blk_f = blk.astype(F32)
            for h in range(hb):
                gsc = lax.dot_general(q_ref[0, h].astype(BF16), aux_ref[0, h].astype(BF16),
                                      (((1,), (1,)), ((), ())), preferred_element_type=F32)
                picked = _topk_mask(jnp.where(ok, gsc, NEG), MOBA_TOPK, blk_f)
                sel_ref[h] = jnp.where(ok, picked, 0.0)

    def step(phase):
        rel = lax.broadcasted_iota(jnp.int32, (tq, tk), 0) - lax.broadcasted_iota(jnp.int32, (tq, tk), 1)
        if mode == "sel":
            flags = aux_ref[0, 0].astype(BF16)
            nbf = flags.shape[1]
            blk = lax.broadcasted_iota(jnp.int32, (nbf, tk), 0)
            kb = (k0 + lax.broadcasted_iota(jnp.int32, (nbf, tk), 1)) >> 6
            picked = jnp.dot(flags, (blk == kb).astype(BF16), preferred_element_type=F32)
        if mode == "moba":
            own_f = jnp.where(((q0 + lax.broadcasted_iota(jnp.int32, (tq, 1), 0)) >> 8) == (k0 >> 8), 1.0, 0.0)
        dist = rel + (q0 - k0)
        base_ok = dist >= 0
        if mode == "win":
            base_ok = base_ok & (dist < WINDOW)
        if mode == "sel":
            base_ok = base_ok & (picked > 0.5)
        heads, scores, valids, values = [], [], [], []
        for hk in range(hb // g):
            kb16 = k_ref[0, hk].astype(BF16)
            vb16 = v_ref[0, hk].astype(BF16)
            for gi in range(g):
                h = hk * g + gi
                qs = (q_ref[0, h] * SCALE).astype(BF16)
                s = lax.dot_general(qs, kb16, (((1,), (1,)), ((), ())), preferred_element_type=F32)
                if bias_kind == "toeplitz":
                    off = pl.multiple_of(dmax - (q0 - k0), LANES)
                    s = s + _toeplitz_rows(bias_ref[h, :, pl.ds(off, tq + tk)], 0, tq, tk)
                elif bias_kind == "table":
                    s = s + bias_ref[h]
                if mode == "fox":
                    s = (s + cq_ref[0, h]) - aux_ref[0, h]
                if mode == "moba":
                    sel = sel_ref[h]
                    lane = lax.broadcasted_iota(jnp.int32, sel.shape, 1)
                    selcol = jnp.sum(jnp.where(lane == (k0 >> 8), sel, 0.0), axis=1, keepdims=True)
                    valid = (own_f * jnp.where(base_ok, 1.0, 0.0) + (1.0 - own_f) * selcol) > 0.5
                else:
                    valid = base_ok
                heads.append(h)
                scores.append(jnp.where(valid, s, NEG))
                valids.append(valid)
                values.append(vb16)
        if phase == "weighted":
            for h, s, valid, vb16 in zip(heads, scores, valids, values):
                p = jnp.where(valid, jnp.exp(s - m_ref[h]), 0.0) / jnp.maximum(l_ref[h], 1e-30)
                acc_ref[h] = acc_ref[h] + jnp.dot(p.astype(BF16), vb16, preferred_element_type=F32)
            return
        m_prev = [m_ref[h] for h in heads]
        m_new = [jnp.maximum(mp, jnp.max(s, axis=1, keepdims=True)) for mp, s in zip(m_prev, scores)]
        probs = [jnp.exp(s - mn) for s, mn in zip(scores, m_new)]
        alphas = [jnp.exp(mp - mn) for mp, mn in zip(m_prev, m_new)]
        sums = [jnp.sum(p, axis=1, keepdims=True) for p in probs]
        if phase == "online":
            pvs = [jnp.dot(p.astype(BF16), vb16, preferred_element_type=F32) for p, vb16 in zip(probs, values)]
        for i, h in enumerate(heads):
            l_ref[h] = alphas[i] * l_ref[h] + sums[i]
            m_ref[h] = m_new[i]
            if phase == "online":
                acc_ref[h] = alphas[i] * acc_ref[h] + pvs[i]

    if two_pass:

        @pl.when(kk < nk)
        def _stats():
            step("stats")

        @pl.when(kk >= nk)
        def _weighted():
            step("weighted")

        @pl.when(kk == 2 * nk - 1)
        def _fin2():
            for h in range(hb):
                o_ref[0, h] = acc_ref[h]
    else:
        @pl.when(kactive(qi, kk))
        def _step():
            step("online")

        @pl.when(kk == klast(qi))
        def _fin():
            for h in range(hb):
                o_ref[0, h] = acc_ref[h] / jnp.maximum(l_ref[h], 1e-30)


def _flash(mode, q, k, v, *, tq, tk, hb, g=1, qbase=0, kbase=0, bias=None, bias_kind=None, aux=None, cq=None,
           two_pass=False, n_heads=None, q_hoff=0, k_hoff=0, v_hoff=0):
    B, _, Tq, _ = q.shape
    H = n_heads or q.shape[1]
    Lk = k.shape[2]
    nq = Tq // tq
    hkb = hb // g
    qo, ko, vo = q_hoff // hb, k_hoff // hkb, v_hoff // hkb
    dmax = 0
    if mode == "win" and Lk > tk:
        nk = WINDOW // tk + tq // tk
        ktile = lambda qi, kk: jnp.maximum(qi * (tq // tk) - WINDOW // tk + kk, 0)
        kactive = lambda qi, kk: qi * (tq // tk) - WINDOW // tk + kk >= 0
        klast = lambda qi: nk - 1
    elif mode == "win":
        nk = 1
        ktile = lambda qi, kk: kk
        kactive = lambda qi, kk: kk >= 0
        klast = lambda qi: 0
    else:
        nk = Lk // tk
        last = lambda qi: jnp.minimum((qbase + qi * tq + tq - 1 - kbase) // tk, nk - 1)
        ktile = lambda qi, kk: jnp.minimum(kk, last(qi))
        kactive = lambda qi, kk: kk <= last(qi)
        klast = last
    if bias_kind == "toeplitz":
        dmax = bias.shape[2] - (tq + tk)
    vtile = ktile
    nsteps = nk
    if two_pass:
        assert nq == 1 and (mode == "win" or (qbase + tq - 1 - kbase) // tk >= nk - 1)
        ktile = lambda qi, kk: kk % nk
        vtile = lambda qi, kk: jnp.maximum(kk - nk, 0)
        nsteps = 2 * nk

    pair_tables = mode != "win" and not two_pass and nq > 1
    if pair_tables:
        pairs = [(qi, kk) for qi in range(nq)
                 for kk in range(min((qbase + qi * tq + tq - 1 - kbase) // tk, nk - 1) + 1)]
        tables = [jnp.asarray([pr[i] for pr in pairs], jnp.int32) for i in range(2)]
        im = lambda fn: (lambda b, hg, t, qt, kt: fn(b, hg, qt[t], kt[t]))
        grid = (B, H // hb, len(pairs))
        sem = ("parallel", "parallel", "arbitrary")
    else:
        tables = []
        im = lambda fn: fn
        grid = (B, H // hb, nq, nsteps)
        sem = ("parallel", "parallel", "parallel", "arbitrary")

    kv_spec = pl.BlockSpec((1, hkb, tk, HEAD_DIM), im(lambda b, hg, qi, kk: (b, ko + hg, ktile(qi, kk), 0)))
    v_spec = pl.BlockSpec((1, hkb, tk, HEAD_DIM), im(lambda b, hg, qi, kk: (b, vo + hg, vtile(qi, kk), 0)))
    in_specs = [pl.BlockSpec((1, hb, tq, HEAD_DIM), im(lambda b, hg, qi, kk: (b, qo + hg, qi, 0))), kv_spec, v_spec]
    args = [q, k, v]
    if bias_kind == "toeplitz":
        in_specs.append(pl.BlockSpec((hb, 1, bias.shape[2]), im(lambda b, hg, qi, kk: (hg, 0, 0))))
        args.append(bias)
    elif bias_kind == "table":
        in_specs.append(pl.BlockSpec((hb, tq, tk), im(lambda b, hg, qi, kk: (hg, qi, ktile(qi, kk)))))
        args.append(bias)
    scratch = [pltpu.VMEM((hb, tq, 1), F32), pltpu.VMEM((hb, tq, 1), F32), pltpu.VMEM((hb, tq, HEAD_DIM), F32)]
    if mode == "moba":
        nb = aux.shape[2]
        in_specs.append(pl.BlockSpec((1, hb, nb, HEAD_DIM), im(lambda b, hg, qi, kk: (b, hg, 0, 0))))
        args.append(aux)
        scratch.append(pltpu.VMEM((hb, tq, nb), F32))
    elif mode == "fox":
        in_specs.append(pl.BlockSpec((1, hb, 1, tk), im(lambda b, hg, qi, kk: (b, hg, 0, ktile(qi, kk)))))
        in_specs.append(pl.BlockSpec((1, hb, tq, 1), im(lambda b, hg, qi, kk: (b, hg, qi, 0))))
        args += [aux, cq]
    elif mode == "sel":
        in_specs.append(pl.BlockSpec((1, 1, tq, aux.shape[3]), im(lambda b, hg, qi, kk: (b, hg, qi, 0))))
        args.append(aux)
    grid_spec = pltpu.PrefetchScalarGridSpec(
        num_scalar_prefetch=len(tables), grid=grid, in_specs=in_specs,
        out_specs=pl.BlockSpec((1, hb, tq, HEAD_DIM), im(lambda b, hg, qi, kk: (b, hg, qi, 0))),
        scratch_shapes=scratch)
    return pl.pallas_call(
        functools.partial(_flash_kernel, mode=mode, bias_kind=bias_kind, hb=hb, g=g, tq=tq, tk=tk,
                          qbase=qbase, kbase=kbase, dmax=dmax, ktile=ktile, kactive=kactive, klast=klast,
                          two_pass=two_pass, nk=nk, pair_tables=pair_tables),
        out_shape=jax.ShapeDtypeStruct((B, H, Tq, HEAD_DIM), F32),
        grid_spec=grid_spec,
        compiler_params=_cparams(sem),
        name=f"flash_{mode}_{tq}",
    )(*tables, *args)


def _compress_kernel(x_ref, pe_ref, wab_ref, w2_ref, o_ref):
    nch = x_ref.shape[2]
    half = CMP_STRIDE * HEAD_DIM
    x = x_ref[0, 0]
    wab = wab_ref[...].astype(BF16)
    first = jnp.dot((x + pe_ref[:, :half]).astype(BF16), wab[:, :HEAD_DIM], preferred_element_type=F32)
    second = jnp.dot((x + pe_ref[:, half:]).astype(BF16), wab[:, HEAD_DIM:], preferred_element_type=F32)
    hmid = first + pltpu.roll(second, nch - 1, 0)
    o_ref[0, 0] = jnp.dot(jax.nn.silu(hmid).astype(BF16), w2_ref[...].astype(BF16), preferred_element_type=F32)


def _compress(x, slot0, nch, pe, w1, w2):
    B = x.shape[0]
    half = CMP_STRIDE * HEAD_DIM
    wab = jnp.concatenate([w1[:CMP_STRIDE].reshape(half, HEAD_DIM), w1[CMP_STRIDE:].reshape(half, HEAD_DIM)], axis=1)
    return pl.pallas_call(
        _compress_kernel,
        out_shape=jax.ShapeDtypeStruct((B, NSA_KV_HEADS, nch, HEAD_DIM), F32),
        grid=(B, NSA_KV_HEADS),
        in_specs=[pl.BlockSpec((1, 1, nch, half), lambda b, h: (b, slot0 + h, 0, 0)),
                  pl.BlockSpec((1, 2 * half), lambda b, h: (0, 0)),
                  pl.BlockSpec((half, 2 * HEAD_DIM), lambda b, h: (0, 0)),
                  pl.BlockSpec((HEAD_DIM, HEAD_DIM), lambda b, h: (0, 0))],
        out_specs=pl.BlockSpec((1, 1, nch, HEAD_DIM), lambda b, h: (b, h, 0, 0)),
        compiler_params=_cparams(("parallel", "parallel")),
        name="nsa_compress",
    )(x, pe.reshape(1, 2 * half), wab, w2)


def _split3(x):
    hi = x.astype(BF16)
    r1 = x - hi.astype(F32)
    mid = r1.astype(BF16)
    lo = (r1 - mid.astype(F32)).astype(BF16)
    return hi, mid, lo


def _cmp_kernel(q_ref, kc_ref, vc_ref, o_ref, f_ref, *, tq, qbase, nbf):
    qi = pl.program_id(2)
    q0 = qbase + qi * tq
    nc1 = kc_ref.shape[2]
    kc = kc_ref[0, 0].astype(BF16)
    vc = vc_ref[0, 0].astype(BF16)
    qpos = q0 + lax.broadcasted_iota(jnp.int32, (tq, nc1), 0)
    n = lax.broadcasted_iota(jnp.int32, (tq, nc1), 1)
    valid = (n * CMP_STRIDE + (CMP_LEN - 1) <= qpos) & (n < nc1 - 1)
    psum = jnp.zeros((tq, nc1), F32)
    for gi in range(NSA_GROUP):
        qs = (q_ref[0, gi] * SCALE).astype(BF16)
        s = lax.dot_general(qs, kc, (((1,), (1,)), ((), ())), preferred_element_type=F32)
        s = jnp.where(valid, s, NEG)
        m = jnp.max(s, axis=1, keepdims=True)
        p = jnp.where(valid, jnp.exp(s - m), 0.0)
        p = p / jnp.maximum(jnp.sum(p, axis=1, keepdims=True), 1e-30)
        o_ref[0, gi] = jnp.dot(p.astype(BF16), vc, preferred_element_type=F32)
        psum = psum + p
    rn = lax.broadcasted_iota(jnp.int32, (nc1, nbf), 0)
    cj = lax.broadcasted_iota(jnp.int32, (nc1, nbf), 1)
    r = SEL_BLOCK // CMP_STRIDE
    ov = ((rn >= r * cj - 1) & (rn <= r * cj + r - 1) & (rn < nc1 - 1)).astype(BF16)
    pslc = jnp.zeros((tq, nbf), F32)
    for part in _split3(psum):
        pslc = pslc + jnp.dot(part, ov, preferred_element_type=F32)
    blk = lax.broadcasted_iota(jnp.int32, (tq, nbf), 1)
    own = (q0 + lax.broadcasted_iota(jnp.int32, (tq, nbf), 0)) >> 6
    ok = blk <= own
    forced = (blk == 0) | (blk == own) | (blk == own - 1)
    score = jnp.where(ok, jnp.where(forced, FORCED_SCORE, pslc), NEG)
    picked = _topk_mask(score, SEL_TOPN, blk.astype(F32))
    f_ref[0, 0] = jnp.where(ok, picked, 0.0)


def _cmp_attention(q, kc, vc, *, tq, qbase, nbf):
    B, H, Tq, _ = q.shape
    nc1 = kc.shape[2]
    kv_spec = pl.BlockSpec((1, 1, nc1, HEAD_DIM), lambda b, h, qi: (b, h, 0, 0))
    return pl.pallas_call(
        functools.partial(_cmp_kernel, tq=tq, qbase=qbase, nbf=nbf),
        out_shape=(jax.ShapeDtypeStruct((B, H, Tq, HEAD_DIM), F32),
                   jax.ShapeDtypeStruct((B, NSA_KV_HEADS, Tq, nbf), F32)),
        grid=(B, NSA_KV_HEADS, Tq // tq),
        in_specs=[pl.BlockSpec((1, NSA_GROUP, tq, HEAD_DIM), lambda b, h, qi: (b, h, qi, 0)), kv_spec, kv_spec],
        out_specs=(pl.BlockSpec((1, NSA_GROUP, tq, HEAD_DIM), lambda b, h, qi: (b, h, qi, 0)),
                   pl.BlockSpec((1, 1, tq, nbf), lambda b, h, qi: (b, h, qi, 0))),
        compiler_params=_cparams(("parallel", "parallel", "parallel")),
        name=f"nsa_cmp_{tq}",
    )(q, kc, vc)


def _gather_kernel(pt_ref, cache_ref, new_ref, o_ref, *chunk_refs, n_pages, n_slots, chunk_slots):
    p = pl.program_id(1)
    nh = n_slots // 2

    @pl.when(p < n_pages)
    def _():
        for s in range(0, n_slots, 2):
            pair = cache_ref[0, 0, s // nh, (s % nh):(s % nh) + 2].reshape(2 * HEAD_DIM, PAGE_SIZE).T
            o_ref[0, s] = pair[:, :HEAD_DIM]
            o_ref[0, s + 1] = pair[:, HEAD_DIM:]
        for i, s in enumerate(chunk_slots):
            for l in range(CMP_STRIDE):
                rows = pl.ds(l, PAGE_SIZE // CMP_STRIDE, stride=CMP_STRIDE)
                chunk_refs[0][0, i, :, l * HEAD_DIM:(l + 1) * HEAD_DIM] = o_ref[0, s, rows, :]

    @pl.when(p >= n_pages)
    def _():
        for s in range(n_slots):
            o_ref[0, s] = new_ref[0, :, s * HEAD_DIM:(s + 1) * HEAD_DIM]
        if chunk_slots:
            chunk_refs[0][...] = jnp.zeros_like(chunk_refs[0])


def _gather_pages(cache, page_table, layer, new_rows, chunk_slots=()):
    B, n_pages = page_table.shape
    nh = cache.shape[3]
    n_slots = 2 * nh
    C = n_slots * HEAD_DIM
    extra = new_rows.shape[1] // PAGE_SIZE
    L = (n_pages + extra) * PAGE_SIZE
    lines = PAGE_SIZE // CMP_STRIDE
    out_shape = [jax.ShapeDtypeStruct((B, n_slots, L, HEAD_DIM), F32)]
    out_specs = [pl.BlockSpec((1, n_slots, PAGE_SIZE, HEAD_DIM), lambda b, p, pt: (b, 0, p, 0))]
    if chunk_slots:
        out_shape.append(jax.ShapeDtypeStruct((B, len(chunk_slots), L // CMP_STRIDE, CMP_STRIDE * HEAD_DIM), F32))
        out_specs.append(pl.BlockSpec((1, len(chunk_slots), lines, CMP_STRIDE * HEAD_DIM),
                                      lambda b, p, pt: (b, 0, p, 0)))
    grid_spec = pltpu.PrefetchScalarGridSpec(
        num_scalar_prefetch=1,
        grid=(B, n_pages + extra),
        in_specs=[pl.BlockSpec((1, 1, 2, nh, HEAD_DIM, PAGE_SIZE),
                               lambda b, p, pt: (pt[b, jnp.minimum(p, n_pages - 1)], layer, 0, 0, 0, 0)),
                  pl.BlockSpec((1, PAGE_SIZE, C), lambda b, p, pt: (b, jnp.maximum(p - n_pages, 0), 0))],
        out_specs=out_specs,
    )
    out = pl.pallas_call(
        functools.partial(_gather_kernel, n_pages=n_pages, n_slots=n_slots, chunk_slots=tuple(chunk_slots)),
        out_shape=out_shape,
        grid_spec=grid_spec,
        compiler_params=_cparams(("parallel", "arbitrary")),
        name="gather_pages",
    )(page_table, cache, new_rows)
    return out if chunk_slots else out[0]


PAGES_PER_STEP = 8


def _gather_flat_kernel(pt_ref, *refs):
    o_ref = refs[-1]
    for i, c_ref in enumerate(refs[:-1]):
        o_ref[0, :, i * PAGE_SIZE:(i + 1) * PAGE_SIZE] = c_ref[0, 0]


def _gather_logf(cache, page_table, layer):
    B, n_pages = page_table.shape
    w, ps = cache.shape[2:]
    spec = lambda i: pl.BlockSpec((1, 1, w, ps), lambda b, s, pt: (pt[b, s * PAGES_PER_STEP + i], layer, 0, 0))
    grid_spec = pltpu.PrefetchScalarGridSpec(
        num_scalar_prefetch=1,
        grid=(B, n_pages // PAGES_PER_STEP),
        in_specs=[spec(i) for i in range(PAGES_PER_STEP)],
        out_specs=pl.BlockSpec((1, w, PAGES_PER_STEP * ps), lambda b, s, pt: (b, 0, s)),
    )
    return pl.pallas_call(
        _gather_flat_kernel,
        out_shape=jax.ShapeDtypeStruct((B, w, n_pages * ps), F32),
        grid_spec=grid_spec,
        compiler_params=_cparams(("parallel", "arbitrary")),
        name="gather_logf",
    )(page_table, *([cache] * PAGES_PER_STEP))


def _rel_bucket(dist):
    n = jnp.maximum(dist, 0)
    max_exact = N_BUCKETS // 2
    nf = jnp.maximum(n, 1).astype(F32)
    large = max_exact + (jnp.log(nf / max_exact) / math.log(MAX_DISTANCE / max_exact)
                         * (N_BUCKETS - max_exact)).astype(jnp.int32)
    large = jnp.minimum(large, N_BUCKETS - 1)
    return jnp.where(n < max_exact, n, large)


def _reversed_bias_table(rel_tab, dmax, tq, tk):
    length = dmax + tq + tk
    d = dmax + tq - 1 - jnp.arange(length)
    vals = jnp.where(d >= 0, rel_tab[:, _rel_bucket(d)], 0.0)
    return vals[:, None, :].astype(F32)


def _decode_bias_table(rel_tab, qbase, kbase, tq, lk):
    d = (qbase + jnp.arange(tq))[:, None] - (kbase + jnp.arange(lk))[None, :]
    return jnp.where(d >= 0, rel_tab[:, _rel_bucket(d)], 0.0).astype(F32)


def _to_heads(x, n):
    B, T, _ = x.shape
    return x.reshape(B, T, n, HEAD_DIM).transpose(0, 2, 1, 3)


def _from_heads(x):
    B, H, T, _ = x.shape
    return x.transpose(0, 2, 1, 3).reshape(B * T, H * HEAD_DIM)


def _pad_rows(x, rows):
    return jnp.pad(x, ((0, 0), (0, rows - x.shape[1])) + ((0, 0),) * (x.ndim - 2))


DEC_TQ = 8


def _decode_key_tile(lk):
    return lk // 3 if lk % (3 * LANES) == 0 else lk // 2


def _even_layer(x, B, T, sample, j, p):
    nq = N_HEADS * HEAD_DIM
    w_in = p["w_in_even"][j]
    q = _mm(x, w_in[:, :nq].astype(BF16))
    kv = _mm(x, w_in[:, nq:3 * nq].astype(BF16))
    wf = jnp.pad(w_in[:, 3 * nq:], ((0, 0), (0, LANES - H_FOX))).astype(BF16)
    bf = jnp.pad(p["b_forget"][j], (0, LANES - H_FOX)).reshape(1, LANES)
    logf = _mm(x, wf, bias=bf, act="log_sigmoid")[:, :H_FOX]
    kv3 = kv.reshape(B, T, 2 * nq)
    logf3 = logf.reshape(B, T, H_FOX)
    rel_tab = p["rel_tab"]
    if not sample:
        qh = _to_heads(q.reshape(B, T, nq), N_HEADS)
        kvh = _to_heads(kv3, 2 * N_HEADS)
        kmean = _kmean(kvh, H_MOBA)
        t = MOBA_BLOCK
        rv = _reversed_bias_table(rel_tab[:H_MOBA], T - t, t, t)
        o_a = _flash("moba", qh, kvh, kvh, tq=t, tk=t, hb=4, n_heads=H_MOBA, v_hoff=N_HEADS,
                     bias=rv, bias_kind="toeplitz", aux=kmean)
        c = _cumsum_lanes(logf3.transpose(0, 2, 1)).reshape(B, H_FOX, 1, T)
        tf = min(512, T)
        o_b = _flash("fox", qh, kvh, kvh, tq=tf, tk=tf, hb=2, n_heads=H_FOX, q_hoff=H_MOBA, k_hoff=H_MOBA,
                     v_hoff=N_HEADS + H_MOBA, aux=c, cq=c.reshape(B, H_FOX, T, 1))
    else:
        past = p["page_table"].shape[1] * PAGE_SIZE
        lk = past + MOBA_BLOCK
        new_rows = _pad_rows(kv3, MOBA_BLOCK)
        kvh = _gather_pages(p["cache_even_kv"], p["page_table"], j, new_rows)
        qh = _to_heads(_pad_rows(q.reshape(B, T, nq), DEC_TQ), N_HEADS)
        kmean = _kmean(kvh, H_MOBA)
        tb = _decode_bias_table(rel_tab[:H_MOBA], past, 0, DEC_TQ, lk)
        o_a = _flash("moba", qh, kvh, kvh, tq=DEC_TQ, tk=MOBA_BLOCK, hb=8, n_heads=H_MOBA, v_hoff=N_HEADS,
                     qbase=past, bias=tb, bias_kind="table", aux=kmean, two_pass=True)
        lf_past = _gather_logf(p["cache_even_logf"], p["page_table"], j)
        lf_all = jnp.concatenate([lf_past, _pad_rows(logf3, MOBA_BLOCK).transpose(0, 2, 1)], axis=2)
        c = _cumsum_lanes(lf_all).reshape(B, H_FOX, 1, lk)
        tkf = _decode_key_tile(lk)
        o_b = _flash("fox", qh, kvh, kvh, tq=DEC_TQ, tk=tkf, hb=2, n_heads=H_FOX, q_hoff=H_MOBA, k_hoff=H_MOBA,
                     v_hoff=N_HEADS + H_MOBA, qbase=past, aux=c,
                     cq=c[:, :, 0, past:past + DEC_TQ].reshape(B, H_FOX, DEC_TQ, 1), two_pass=True)
        o_a, o_b = o_a[:, :, :T], o_b[:, :, :T]
    o = _from_heads(jnp.concatenate([o_a, o_b], axis=1))
    layer = 2 * j
    x1 = _mm_ln([o], None, p["w_out_even"][j].astype(BF16), x, p["ln_g"][layer, 0], p["ln_b"][layer, 0])
    x2 = _ffn(x1, None, p["ffn_w13"][j:j + 1].astype(BF16), p["ffn_w2"][j:j + 1].astype(BF16),
              p["ln_g"][layer, 1], p["ln_b"][layer, 1])
    return x2, kv3, logf3


def _odd_layer(x, B, T, sample, j, p):
    hk, d = NSA_KV_HEADS, HEAD_DIM
    nq = N_HEADS * d
    nkv = 4 * hk * d
    nw = 2 * hk * d
    w_in = p["w_in_odd"][j]
    q = _mm(x, w_in[:, :nq].astype(BF16))
    kv = _mm(x, w_in[:, nq:nq + nkv].astype(BF16))
    kvw = _mm(x, w_in[:, nq + nkv:nq + nkv + nw].astype(BF16), tn=nw)
    ngate = 3 * N_HEADS
    wg = jnp.pad(w_in[:, nq + nkv + nw:], ((0, 0), (0, LANES - ngate))).astype(BF16)
    bg = jnp.pad(p["b_gate"][j], (0, LANES - ngate)).reshape(1, LANES)
    gates = _mm(x, wg, bias=bg, act="sigmoid")
    kv3 = kv.reshape(B, T, nkv)
    kvw3 = kvw.reshape(B, T, nw)
    rel_tab = p["rel_tab"]
    pe, w1, w2 = p["cmp_pe"][j], p["cmp_w1"][j], p["cmp_w2"][j]
    if not sample:
        qh = _to_heads(q.reshape(B, T, nq), N_HEADS)
        kvh = _to_heads(kv3, 4 * hk)
        chunks = kvh.reshape(B, 4 * hk, T // CMP_STRIDE, CMP_STRIDE * d)
        kc = _compress(chunks, 0, T // CMP_STRIDE, pe[0], w1[0], w2[0])
        vc = _compress(chunks, 2 * hk, T // CMP_STRIDE, pe[1], w1[1], w2[1])
        t = 256
        o_cmp, flags = _cmp_attention(qh, kc, vc, tq=t, qbase=0, nbf=T // SEL_BLOCK)
        rv = _reversed_bias_table(rel_tab, T - t, t, t)
        o_sel = _flash("sel", qh, kvh, kvh, tq=t, tk=t, hb=NSA_GROUP, g=NSA_GROUP, k_hoff=hk, v_hoff=3 * hk,
                       bias=rv, bias_kind="toeplitz", aux=flags)
        kwh = _to_heads(kvw3, 2 * hk)
        o_win = _flash("win", qh, kwh, kwh, tq=t, tk=t, hb=NSA_GROUP, g=NSA_GROUP, v_hoff=hk,
                       bias=rv, bias_kind="toeplitz")
    else:
        past = p["page_table"].shape[1] * PAGE_SIZE
        lk = past + 256
        new_rows = _pad_rows(kv3, 256)
        cmp_slots = tuple(range(hk)) + tuple(range(2 * hk, 3 * hk))
        kvh, chunks = _gather_pages(p["cache_odd_kv"], p["page_table"], j, new_rows, cmp_slots)
        qh = _to_heads(_pad_rows(q.reshape(B, T, nq), DEC_TQ), N_HEADS)
        nch = past // CMP_STRIDE
        assert (past + T) // CMP_STRIDE == nch, "new rows must not complete a compression chunk"
        kc = _compress(chunks, 0, nch, pe[0], w1[0], w2[0])
        vc = _compress(chunks, hk, nch, pe[1], w1[1], w2[1])
        nbf = 256
        o_cmp, flags = _cmp_attention(qh, kc, vc, tq=DEC_TQ, qbase=past, nbf=nbf)
        tb = _decode_bias_table(rel_tab, past, 0, DEC_TQ, lk)
        o_sel = _flash("sel", qh, kvh, kvh, tq=DEC_TQ, tk=_decode_key_tile(lk), hb=NSA_GROUP, g=NSA_GROUP, k_hoff=hk,
                       v_hoff=3 * hk, qbase=past, bias=tb, bias_kind="table", aux=flags, two_pass=True)
        win = p["state_odd_win"][:, j]
        wlen = win.shape[1]
        lw = wlen + LANES
        ext = jnp.concatenate([win.reshape(B, wlen, nw), kvw3], axis=1)
        kwh = _to_heads(_pad_rows(ext, lw), 2 * hk)
        tbw = _decode_bias_table(rel_tab, past, past - wlen, DEC_TQ, lw)
        o_win = _flash("win", qh, kwh, kwh, tq=DEC_TQ, tk=lw, hb=NSA_GROUP, g=NSA_GROUP, v_hoff=hk,
                       qbase=past, kbase=past - wlen, bias=tbw, bias_kind="table", two_pass=True)
        o_cmp, o_sel, o_win = o_cmp[:, :, :T], o_sel[:, :, :T], o_win[:, :, :T]
    layer = 2 * j + 1
    x1 = _mm_ln([_from_heads(o_cmp), _from_heads(o_sel), _from_heads(o_win)], gates,
                p["w_out_odd"][j].astype(BF16), x, p["ln_g"][layer, 0], p["ln_b"][layer, 0])
    rg = _router(x1, p["w_router"][j])
    x2 = _ffn(x1, rg, p["moe_w13"][j].astype(BF16), p["moe_w2"][j].astype(BF16),
              p["ln_g"][layer, 1], p["ln_b"][layer, 1])
    return x2, kv3, kvw3


def _run(x3, sample, p):
    B, T, _ = x3.shape
    x = x3.reshape(B * T, D_MODEL)
    ekv, elf, okv, owin = [], [], [], []
    for layer in range(DEPTH):
        j = layer // 2
        if layer % 2 == 0:
            x, kv_new, lf_new = _even_layer(x, B, T, sample, j, p)
            ekv.append(kv_new.reshape(B, T, 2, N_HEADS, HEAD_DIM))
            elf.append(lf_new)
        else:
            x, kv_new, kvw_new = _odd_layer(x, B, T, sample, j, p)
            okv.append(kv_new.reshape(B, T, 2, 2 * NSA_KV_HEADS, HEAD_DIM))
            kvw5 = kvw_new.reshape(B, T, 2, NSA_KV_HEADS, HEAD_DIM)
            if sample:
                win = p["state_odd_win"][:, j]
                owin.append(jnp.concatenate([win, kvw5], axis=1)[:, -win.shape[1]:])
            else:
                owin.append(kvw5[:, -min(WINDOW, T):])
    return (x.reshape(B, T, D_MODEL), jnp.stack(ekv, 1), jnp.stack(elf, 1), jnp.stack(okv, 1), jnp.stack(owin, 1))


def kernel(x_prompt, x_sample, cache_even_kv, cache_even_logf, cache_odd_kv, state_odd_win, page_table, rel_bias, ln_g, ln_b, w_in_even, b_forget, w_out_even, ffn_w13, ffn_w2, w_in_odd, b_gate, cmp_pe, cmp_w1, cmp_w2, w_out_odd, w_router, moe_w13, moe_w2):
    n_pool = cache_even_kv.shape[0]
    p = dict(
        rel_tab=rel_bias.T,
        page_table=page_table,
        cache_even_kv=jnp.transpose(cache_even_kv, (0, 1, 3, 4, 5, 2)),
        cache_even_logf=jnp.transpose(cache_even_logf, (0, 1, 3, 2)),
        cache_odd_kv=jnp.transpose(cache_odd_kv, (0, 1, 3, 4, 5, 2)),
        state_odd_win=state_odd_win,
        ln_g=ln_g, ln_b=ln_b, w_in_even=w_in_even, b_forget=b_forget, w_out_even=w_out_even,
        ffn_w13=ffn_w13, ffn_w2=ffn_w2, w_in_odd=w_in_odd, b_gate=b_gate, cmp_pe=cmp_pe, cmp_w1=cmp_w1,
        cmp_w2=cmp_w2, w_out_odd=w_out_odd, w_router=w_router, moe_w13=moe_w13, moe_w2=moe_w2)
    y_p, p_ekv, p_elf, p_okv, p_win = _run(x_prompt, False, p)
    y_s, s_ekv, s_elf, s_okv, s_win = _run(x_sample, True, p)
    return (y_p, y_s, p_ekv, s_ekv, p_elf, s_elf, p_okv, s_okv, p_win, s_win)
```
